```python
import math
import jax, jax.numpy as jnp
from jax import lax
import numpy as np

D_MODEL = 2048
BATCH = 4
SEQ = 2048
DEPTH = 4
DEC_BATCH = 128
DEC_SEQ = 1
PAST_LEN = 16384
PAGE_SIZE = 128

N_MIXERS = 2
N_CONV_LAYERS = (DEPTH + 1) // N_MIXERS
N_HGRN_LAYERS = DEPTH // N_MIXERS
D_CONV = D_MODEL
CONV_WIDTH = 31
HGRN_EXPAND = 128
HGRN_HEADS = D_MODEL // HGRN_EXPAND
HGRN_DK = HGRN_EXPAND
HGRN_DV = D_MODEL // HGRN_HEADS
CHUNK = 64
D_FF = 5632
FFN_WIDTH = 3
EPS = 1e-6

kernel_name = "hybrid_conformer_hgrn2_convffn_step"


def rmsnorm(x, w):
    xf = x.astype(jnp.float32)
    y = xf * lax.rsqrt(jnp.mean(xf * xf, axis=-1, keepdims=True) + EPS)
    return (y * w.astype(jnp.float32)).astype(x.dtype)


def layernorm(x, w, b):
    xf = x.astype(jnp.float32)
    mu = jnp.mean(xf, axis=-1, keepdims=True)
    var = jnp.mean(jnp.square(xf - mu), axis=-1, keepdims=True)
    y = (xf - mu) * lax.rsqrt(var + EPS)
    return (y * w.astype(jnp.float32) + b.astype(jnp.float32)).astype(x.dtype)


def causal_depthwise_conv(x, buf, w, b):
    xx = jnp.concatenate([buf.astype(x.dtype), x], axis=1)
    c = x.shape[-1]
    y = lax.conv_general_dilated(xx, w.astype(x.dtype)[:, None, :], window_strides=(1,), padding='VALID',
                                 dimension_numbers=('NWC', 'WIO', 'NWC'), feature_group_count=c)
    return y + b.astype(x.dtype), xx[:, -buf.shape[1]:]


def conformer_conv(x, buf, w_pw1, b_pw1, w_dw, b_dw, ln_g, ln_b, w_pw2, b_pw2):
    h = x @ w_pw1 + b_pw1
    a, gt = jnp.split(h, 2, axis=-1)
    u = a * jax.nn.sigmoid(gt)
    c, new_buf = causal_depthwise_conv(u, buf, w_dw, b_dw)
    c = jax.nn.silu(layernorm(c, ln_g, ln_b))
    return c @ w_pw2 + b_pw2, new_buf


def gla_chunked(q, k, v, log_f, S0):
    B, L, H, DK = q.shape
    DV = v.shape[-1]
    C = math.gcd(L, CHUNK)
    NC = L // C

    def to_chunks(t):
        return t.astype(jnp.float32).reshape(B, NC, C, H, t.shape[-1]).transpose(1, 0, 2, 3, 4)

    mask = jnp.tril(jnp.ones((C, C), dtype=bool))

    def step(S, inp):
        qc, kc, vc, gc = inp
        G = jnp.cumsum(gc, axis=1)
        o_inter = jnp.einsum('bthk,bhkv->bthv', qc * jnp.exp(G), S)
        rel = jnp.where(mask[None, :, :, None, None], G[:, :, None] - G[:, None, :], -jnp.inf)
        decay = jnp.exp(rel)
        A = jnp.einsum('bthk,btshk,bshk->bths', qc, decay, kc)
        o_intra = jnp.einsum('bths,bshv->bthv', A, vc)
        G_last = G[:, -1]
        S_new = jnp.exp(G_last)[..., None] * S + jnp.einsum(
            'bshk,bshv->bhkv', kc * jnp.exp(G_last[:, None] - G), vc)
        return S_new, o_inter + o_intra

    S, o = lax.scan(step, S0.astype(jnp.float32), (to_chunks(q), to_chunks(k), to_chunks(v), to_chunks(log_f)))
    o = o.transpose(1, 0, 2, 3, 4).reshape(B, L, H, DV)
    return o, S


def hgrn2_mixer(x, S0, w_q, w_f, w_i, w_g, w_o, lb, norm_g):
    B, L, _ = x.shape
    q = jax.nn.silu(x @ w_q).reshape(B, L, HGRN_HEADS, HGRN_DK)
    z = (x @ w_f).astype(jnp.float32).reshape(B, L, HGRN_HEADS, HGRN_DK)
    lb = lb.reshape(HGRN_HEADS, HGRN_DK)
    log_f = jnp.logaddexp(jnp.log(lb), jnp.log1p(-lb) + jax.nn.log_sigmoid(z))
    k = (1.0 - lb) * jax.nn.sigmoid(-z)
    v = (x @ w_i).reshape(B, L, HGRN_HEADS, HGRN_DV)
    o, S = gla_chunked(q, k, v, log_f, S0)
    o = rmsnorm(o.astype(x.dtype), norm_g.reshape(HGRN_HEADS, HGRN_DV))
    o = o.reshape(B, L, D_MODEL) * jax.nn.silu(x @ w_g)
    return o @ w_o, S.astype(S0.dtype)


def conv_ffn(x, buf, w_up, w_dw, b_dw, w_down):
    h = x @ w_up
    h, new_buf = causal_depthwise_conv(h, buf, w_dw, b_dw)
    gate, up = jnp.split(h, 2, axis=-1)
    return (jax.nn.silu(gate) * up) @ w_down, new_buf


def trunk(x, conv_state, hgrn_state, ffn_state, p):
    (norm_mix, norm_ffn, norm_final,
     conv_w_pw1, conv_b_pw1, conv_w_dw, conv_b_dw, conv_ln_g, conv_ln_b, conv_w_pw2, conv_b_pw2,
     hgrn_w_q, hgrn_w_f, hgrn_w_i, hgrn_w_g, hgrn_w_o, hgrn_lb, hgrn_norm_g,
     ffn_w_up, ffn_w_dw, ffn_b_dw, ffn_w_down) = p
    new_conv, new_hgrn, new_ffn = [], [], []
    for i in range(DEPTH):
        j = i // N_MIXERS
        h = rmsnorm(x, norm_mix[i])
        if i % N_MIXERS == 0:
            y, s = conformer_conv(h, conv_state[j], conv_w_pw1[j], conv_b_pw1[j], conv_w_dw[j], conv_b_dw[j],
                                  conv_ln_g[j], conv_ln_b[j], conv_w_pw2[j], conv_b_pw2[j])
            new_conv.append(s)
        else:
            y, s = hgrn2_mixer(h, hgrn_state[j], hgrn_w_q[j], hgrn_w_f[j], hgrn_w_i[j], hgrn_w_g[j],
                               hgrn_w_o[j], hgrn_lb[j], hgrn_norm_g[j])
            new_hgrn.append(s)
        x = x + y
        h = rmsnorm(x, norm_ffn[i])
        y, s = conv_ffn(h, ffn_state[i], ffn_w_up[i], ffn_w_dw[i], ffn_b_dw[i], ffn_w_down[i])
        new_ffn.append(s)
        x = x + y
    return rmsnorm(x, norm_final), jnp.stack(new_conv), jnp.stack(new_hgrn), jnp.stack(new_ffn)


def setup_inputs(seed: int = 0) -> dict:
    key = jax.random.key(seed)
    ks = jax.random.split(key, 32)

    def nrm(k, shape, scale):
        return jax.random.normal(k, shape, dtype=jnp.float32) * scale

    D = D_MODEL
    return {
        'x_prompt': nrm(ks[0], (BATCH, SEQ, D), 1.0),
        'x_sample': nrm(ks[1], (DEC_BATCH, DEC_SEQ, D), 1.0),
        'state_conv': nrm(ks[2], (N_CONV_LAYERS, DEC_BATCH, CONV_WIDTH - 1, D_CONV), 0.5),
        'state_hgrn': nrm(ks[3], (N_HGRN_LAYERS, DEC_BATCH, HGRN_HEADS, HGRN_DK, HGRN_DV), 0.3),
        'state_ffn': nrm(ks[4], (DEPTH, DEC_BATCH, FFN_WIDTH - 1, 2 * D_FF), 1.0),
        'norm_mix': 1.0 + nrm(ks[5], (DEPTH, D), 0.02),
        'norm_ffn': 1.0 + nrm(ks[6], (DEPTH, D), 0.02),
        'norm_final': 1.0 + nrm(ks[7], (D,), 0.02),
        'conv_w_pw1': nrm(ks[8], (N_CONV_LAYERS, D, 2 * D_CONV), D ** -0.5),
        'conv_b_pw1': nrm(ks[9], (N_CONV_LAYERS, 2 * D_CONV), 0.02),
        'conv_w_dw': nrm(ks[10], (N_CONV_LAYERS, CONV_WIDTH, D_CONV), CONV_WIDTH ** -0.5),
        'conv_b_dw': nrm(ks[11], (N_CONV_LAYERS, D_CONV), 0.02),
        'conv_ln_g': 1.0 + nrm(ks[12], (N_CONV_LAYERS, D_CONV), 0.02),
        'conv_ln_b': nrm(ks[13], (N_CONV_LAYERS, D_CONV), 0.02),
        'conv_w_pw2': nrm(ks[14], (N_CONV_LAYERS, D_CONV, D), D_CONV ** -0.5),
        'conv_b_pw2': nrm(ks[15], (N_CONV_LAYERS, D), 0.02),
        'hgrn_w_q': nrm(ks[16], (N_HGRN_LAYERS, D, HGRN_HEADS * HGRN_DK), D ** -0.5),
        'hgrn_w_f': nrm(ks[17], (N_HGRN_LAYERS, D, HGRN_HEADS * HGRN_DK), D ** -0.5),
        'hgrn_w_i': nrm(ks[18], (N_HGRN_LAYERS, D, HGRN_HEADS * HGRN_DV), D ** -0.5),
        'hgrn_w_g': nrm(ks[19], (N_HGRN_LAYERS, D, D), D ** -0.5),
        'hgrn_w_o': nrm(ks[20], (N_HGRN_LAYERS, D, D), D ** -0.5),
        'hgrn_lb_raw': nrm(ks[21], (N_HGRN_LAYERS, HGRN_HEADS * HGRN_DK), 1.0),
        'hgrn_norm_g': 1.0 + nrm(ks[22], (N_HGRN_LAYERS, D), 0.02),
        'ffn_w_up': nrm(ks[23], (DEPTH, D, 2 * D_FF), D ** -0.5),
        'ffn_w_dw': nrm(ks[24], (DEPTH, FFN_WIDTH, 2 * D_FF), FFN_WIDTH ** -0.5),
        'ffn_b_dw': nrm(ks[25], (DEPTH, 2 * D_FF), 0.02),
        'ffn_w_down': nrm(ks[26], (DEPTH, D_FF, D), D_FF ** -0.5),
    }


def reference(x_prompt, x_sample, state_conv, state_hgrn, state_ffn,
              norm_mix, norm_ffn, norm_final,
              conv_w_pw1, conv_b_pw1, conv_w_dw, conv_b_dw, conv_ln_g, conv_ln_b, conv_w_pw2, conv_b_pw2,
              hgrn_w_q, hgrn_w_f, hgrn_w_i, hgrn_w_g, hgrn_w_o, hgrn_lb_raw, hgrn_norm_g,
              ffn_w_up, ffn_w_dw, ffn_b_dw, ffn_w_down):
    lb_cum = jnp.cumsum(jax.nn.softmax(hgrn_lb_raw.astype(jnp.float32), axis=0), axis=0)
    hgrn_lb = lb_cum - lb_cum[0:1]
    params = (norm_mix, norm_ffn, norm_final,
              conv_w_pw1, conv_b_pw1, conv_w_dw, conv_b_dw, conv_ln_g, conv_ln_b, conv_w_pw2, conv_b_pw2,
              hgrn_w_q, hgrn_w_f, hgrn_w_i, hgrn_w_g, hgrn_w_o, hgrn_lb, hgrn_norm_g,
              ffn_w_up, ffn_w_dw, ffn_b_dw, ffn_w_down)
    dt = x_prompt.dtype
    zero_conv = jnp.zeros((N_CONV_LAYERS, BATCH, CONV_WIDTH - 1, D_CONV), dtype=dt)
    zero_hgrn = jnp.zeros((N_HGRN_LAYERS, BATCH, HGRN_HEADS, HGRN_DK, HGRN_DV), dtype=dt)
    zero_ffn = jnp.zeros((DEPTH, BATCH, FFN_WIDTH - 1, 2 * D_FF), dtype=dt)
    y_prompt, conv_p, hgrn_p, ffn_p = trunk(x_prompt, zero_conv, zero_hgrn, zero_ffn, params)
    y_sample, conv_s, hgrn_s, ffn_s = trunk(x_sample, state_conv, state_hgrn, state_ffn, params)
    return (y_prompt, y_sample, conv_p, hgrn_p, ffn_p, conv_s, hgrn_s, ffn_s)
```

```python
import functools

import jax
import jax.numpy as jnp
from jax import lax
from jax.experimental import pallas as pl
from jax.experimental.pallas import tpu as pltpu

F32 = jnp.float32
BF16 = jnp.bfloat16
EPS = 1e-6

V7X_VMEM_BYTES = 64 * 1024 * 1024
V7X_LANES = 128
V7X_SUBLANES = 8
VMEM_LIMIT = V7X_VMEM_BYTES - 8 * 1024 * 1024

GLA_CHUNK = 64
GLA_SUB = 16
FFN_TM = 512
FFN_TF = 512
CONV_TL = 256
PROJ_TM = 512


def _params(n_axes, vmem=VMEM_LIMIT):
    return pltpu.CompilerParams(
        dimension_semantics=("arbitrary",) * n_axes, vmem_limit_bytes=vmem)


def _resident(shape):
    n = len(shape)
    return pl.BlockSpec(shape, lambda *_: (0,) * n, pipeline_mode=pl.Buffered(1))


def _rms_bf16(x, w):
    ms = jnp.mean(x * x, axis=-1, keepdims=True)
    return (x * lax.rsqrt(ms + EPS) * w).astype(BF16)


def _silu(x):
    return x * jax.nn.sigmoid(x)


def _dot(a, b):
    return jnp.dot(a, b, preferred_element_type=F32)


def _dot_nt(a, b):
    return lax.dot_general(a, b, (((1,), (1,)), ((), ())), preferred_element_type=F32)


def _dot_tn(a, b):
    return lax.dot_general(a, b, (((0,), (0,)), ((), ())), preferred_element_type=F32)


def _ffn_kernel(*refs, prompt, tm, tiles_per_seq, final_norm, nf):
    if prompt:
        (x_ref, nw_ref, wg_ref, wu_ref, cw_ref, cb_ref, wd_ref, fw_ref,
         o_ref, st_ref, xn_ref, hs_ref, carry_ref) = refs
    else:
        (x_ref, nw_ref, wg_ref, wu_ref, cw_ref, cb_ref, wd_ref, fw_ref, prev_ref,
         o_ref, hn_ref, xn_ref) = refs
    m = pl.program_id(0)
    f = pl.program_id(1)

    @pl.when(f == 0)
    def _():
        x = x_ref[...]
        xn_ref[...] = _rms_bf16(x, nw_ref[...])
        o_ref[...] = x

    xn = xn_ref[...]
    ys = []
    for half, w_ref in ((0, wg_ref), (1, wu_ref)):
        h = _dot(xn, w_ref[...])
        if prompt:
            seq_start = (m % tiles_per_seq) == 0
            prev = carry_ref[f, half]
            hs_ref[half, 0:V7X_SUBLANES, :] = jnp.where(seq_start, 0.0, prev)
            hs_ref[half, V7X_SUBLANES:, :] = h
            carry_ref[f, half] = h[tm - V7X_SUBLANES:, :]
            tf = h.shape[1]
            cols = pl.ds(pl.multiple_of(f * tf, tf), tf)
            for r in range(2):
                st_ref[0, r, half:half + 1, cols] = h[tm - 2 + r:tm - 1 + r, :]
            h1 = hs_ref[half, V7X_SUBLANES - 1:V7X_SUBLANES - 1 + tm, :]
            h2 = hs_ref[half, V7X_SUBLANES - 2:V7X_SUBLANES - 2 + tm, :]
        else:
            hn_ref[half] = h
            h2 = prev_ref[0, half]
            h1 = prev_ref[1, half]
        y = (cw_ref[2, half:half + 1, :] * h + cw_ref[1, half:half + 1, :] * h1
             + cw_ref[0, half:half + 1, :] * h2 + cb_ref[half:half + 1, :])
        ys.append(y)
    p = (_silu(ys[0]) * ys[1]).astype(BF16)
    o_ref[...] += _dot(p, wd_ref[...])

    if final_norm:
        @pl.when(f == nf - 1)
        def _():
            o = o_ref[...]
            ms = jnp.mean(o * o, axis=-1, keepdims=True)
            o_ref[...] = o * lax.rsqrt(ms + EPS) * fw_ref[...]


def _ffn(x2d, seq_len, nw, w_up, w_dw, b_dw, w_down, fw, prev=None, final_norm=False):
    T, D = x2d.shape
    F = w_down.shape[0]
    prompt = prev is None
    tm = min(FFN_TM, seq_len) if prompt else T
    tf = FFN_TF if F % FFN_TF == 0 else F
    nf = F // tf
    nm = T // tm
    cw = w_dw.reshape(3, 2, F)
    cb = b_dw.reshape(2, F)
    in_specs = [
        pl.BlockSpec((tm, D), lambda m, f: (m, 0)),
        pl.BlockSpec((1, D), lambda m, f: (0, 0)),
        pl.BlockSpec((D, tf), lambda m, f: (0, f)),
        pl.BlockSpec((D, tf), lambda m, f: (0, nf + f)),
        pl.BlockSpec((3, 2, tf), lambda m, f: (0, 0, f)),
        pl.BlockSpec((2, tf), lambda m, f: (0, f)),
        pl.BlockSpec((tf, D), lambda m, f: (f, 0)),
        pl.BlockSpec((1, D), lambda m, f: (0, 0)),
    ]
    args = [x2d, nw.reshape(1, D), w_up, w_up, cw, cb, w_down, fw.reshape(1, D)]
    scratch = [pltpu.VMEM((tm, D), BF16)]
    if prompt:
        tiles_per_seq = seq_len // tm
        nb = T // seq_len
        out_shape = (jax.ShapeDtypeStruct((T, D), F32),
                     jax.ShapeDtypeStruct((nb, 2, 2, F), F32))
        out_specs = (pl.BlockSpec((tm, D), lambda m, f: (m, 0)),
                     pl.BlockSpec((1, 2, 2, F), lambda m, f: (m // tiles_per_seq, 0, 0, 0)))
        scratch += [pltpu.VMEM((2, tm + V7X_SUBLANES, tf), F32),
                    pltpu.VMEM((nf, 2, V7X_SUBLANES, tf), F32)]
    else:
        tiles_per_seq = 1
        in_specs.append(pl.BlockSpec((2, 2, tm, tf), lambda m, f: (0, 0, m, f)))
        args.append(prev)
        out_shape = (jax.ShapeDtypeStruct((T, D), F32),
                     jax.ShapeDtypeStruct((2, T, F), F32))
        out_specs = (pl.BlockSpec((tm, D), lambda m, f: (m, 0)),
                     pl.BlockSpec((2, tm, tf), lambda m, f: (0, m, f)))
    kern = functools.partial(_ffn_kernel, prompt=prompt, tm=tm, tiles_per_seq=tiles_per_seq,
                             final_norm=final_norm, nf=nf)
    return pl.pallas_call(
        kern, out_shape=out_shape, grid=(nm, nf), in_specs=in_specs, out_specs=out_specs,
        scratch_shapes=scratch, compiler_params=_params(2),
        name="ffn_prompt" if prompt else "ffn_sample")(*args)


def _ln_silu_bf16(c, g, b):
    mu = jnp.mean(c, axis=-1, keepdims=True)
    d = c - mu
    var = jnp.mean(d * d, axis=-1, keepdims=True)
    return _silu(d * lax.rsqrt(var + EPS) * g + b).astype(BF16)


def _conv_prompt_kernel(x_ref, nw_ref, w1_ref, b1_ref, cw_ref, cb_ref, lg_ref, lb_ref,
                        w2_ref, b2_ref, o_ref, st_ref, us_ref, c_ref, *, tl, nt, dc, width):
    t = pl.program_id(1)
    halo = 32
    lead = halo - (width - 1)
    x = x_ref[...]
    xn = _rms_bf16(x, nw_ref[...])
    h = _dot(xn, w1_ref[...]) + b1_ref[...]
    u = h[:, :dc] * jax.nn.sigmoid(h[:, dc:])

    @pl.when(t == 0)
    def _():
        us_ref[0:halo, :] = jnp.zeros((halo, dc), F32)

    us_ref[halo:, :] = u

    rb = min(64, tl)
    cwid = V7X_LANES

    def col_body(ci, carry):
        c0 = pl.multiple_of(ci * cwid, cwid)
        cols = pl.ds(c0, cwid)
        for r0 in range(0, tl, rb):
            acc = jnp.broadcast_to(cb_ref[:, cols], (rb, cwid))
            for a in range(V7X_SUBLANES):
                offs = [lead + j for j in range(width) if (lead + j) % V7X_SUBLANES == a]
                if not offs:
                    continue
                base = offs[0]
                span = rb + offs[-1] - base
                y = us_ref[pl.ds(r0 + base, span), cols]
                for off in offs:
                    j = off - lead
                    acc = acc + cw_ref[j:j + 1, cols] * y[off - base:off - base + rb, :]
            c_ref[pl.ds(r0, rb), cols] = acc
        return carry

    lax.fori_loop(0, dc // cwid, col_body, 0)

    @pl.when(t == nt - 1)
    def _():
        st_ref[0] = us_ref[halo + tl - (width - 1):halo + tl, :]

    us_ref[0:halo, :] = u[tl - halo:, :]

    c = _ln_silu_bf16(c_ref[...], lg_ref[...], lb_ref[...])
    o_ref[...] = x + _dot(c, w2_ref[...]) + b2_ref[...]


def _conv_prompt(x2d, nb, seq_len, nw, w1, b1, cw, cb, lg, lb, w2, b2):
    T, D = x2d.shape
    dc = w2.shape[0]
    width = cw.shape[0]
    tl = min(CONV_TL, seq_len)
    nt = seq_len // tl
    row = lambda v: v.reshape(1, -1)
    kern = functools.partial(_conv_prompt_kernel, tl=tl, nt=nt, dc=dc, width=width)
    const2 = lambda shape: pl.BlockSpec(shape, lambda b, t: (0, 0))
    return pl.pallas_call(
        kern,
        out_shape=(jax.ShapeDtypeStruct((T, D), F32),
                   jax.ShapeDtypeStruct((nb, width - 1, dc), F32)),
        grid=(nb, nt),
        in_specs=[
            pl.BlockSpec((tl, D), lambda b, t: (b * nt + t, 0)),
            const2((1, D)),
            _resident((D, 2 * dc)),
            const2((1, 2 * dc)),
            const2((width, dc)),
            const2((1, dc)), const2((1, dc)), const2((1, dc)),
            _resident((dc, D)),
            const2((1, D)),
        ],
        out_specs=(pl.BlockSpec((tl, D), lambda b, t: (b * nt + t, 0)),
                   pl.BlockSpec((1, width - 1, dc), lambda b, t: (b, 0, 0))),
        scratch_shapes=[pltpu.VMEM((32 + tl, dc), F32), pltpu.VMEM((tl, dc), F32)],
        compiler_params=_params(2), name="conv_prompt",
    )(x2d, row(nw), w1, row(b1), cw, row(cb), row(lg), row(lb), w2, row(b2))


def _conv_sample_a_kernel(x_ref, nw_ref, wa_ref, wg_ref, ba_ref, bg_ref, cw_ref, cb_ref,
                          st_ref, c_ref, ns_ref, xn_ref, *, width):
    n = pl.program_id(0)

    @pl.when(n == 0)
    def _():
        xn_ref[...] = _rms_bf16(x_ref[...], nw_ref[...])

    xn = xn_ref[...]
    u = (_dot(xn, wa_ref[...]) + ba_ref[...]) * jax.nn.sigmoid(_dot(xn, wg_ref[...]) + bg_ref[...])
    acc = cb_ref[...] + cw_ref[width - 1:width, :] * u
    for j in range(width - 1):
        row = st_ref[:, j, :]
        acc = acc + cw_ref[j:j + 1, :] * row
        if j > 0:
            ns_ref[:, j - 1, :] = row
    ns_ref[:, width - 2, :] = u
    c_ref[...] = acc


def _conv_sample_b_kernel(c_ref, x_ref, lg_ref, lb_ref, w2_ref, b2_ref, o_ref, cn_ref):
    n = pl.program_id(0)

    @pl.when(n == 0)
    def _():
        cn_ref[...] = _ln_silu_bf16(c_ref[...], lg_ref[...], lb_ref[...])

    o_ref[...] = x_ref[...] + _dot(cn_ref[...], w2_ref[...]) + b2_ref[...]


def _conv_sample(x2d, state, nw, w1, b1, cw, cb, lg, lb, w2, b2):
    NB, D = x2d.shape
    dc = w2.shape[0]
    width = cw.shape[0]
    tn = min(256, dc)
    nn = dc // tn
    row = lambda v: v.reshape(1, -1)
    c, new_state = pl.pallas_call(
        functools.partial(_conv_sample_a_kernel, width=width),
        out_shape=(jax.ShapeDtypeStruct((NB, dc), F32),
                   jax.ShapeDtypeStruct((NB, width - 1, dc), F32)),
        grid=(nn,),
        in_specs=[
            pl.BlockSpec((NB, D), lambda n: (0, 0)),
            pl.BlockSpec((1, D), lambda n: (0, 0)),
            pl.BlockSpec((D, tn), lambda n: (0, n)),
            pl.BlockSpec((D, tn), lambda n: (0, nn + n)),
            pl.BlockSpec((1, tn), lambda n: (0, n)),
            pl.BlockSpec((1, tn), lambda n: (0, nn + n)),
            pl.BlockSpec((width, tn), lambda n: (0, n)),
            pl.BlockSpec((1, tn), lambda n: (0, n)),
            pl.BlockSpec((NB, width - 1, tn), lambda n: (0, 0, n)),
        ],
        out_specs=(pl.BlockSpec((NB, tn), lambda n: (0, n)),
                   pl.BlockSpec((NB, width - 1, tn), lambda n: (0, 0, n))),
        scratch_shapes=[pltpu.VMEM((NB, D), BF16)],
        compiler_params=_params(1), name="conv_sample_a",
    )(x2d, row(nw), w1, w1, row(b1), row(b1), cw, row(cb), state)
    tn2 = min(512, D)
    y = pl.pallas_call(
        _conv_sample_b_kernel,
        out_shape=jax.ShapeDtypeStruct((NB, D), F32),
        grid=(D // tn2,),
        in_specs=[
            pl.BlockSpec((NB, dc), lambda n: (0, 0)),
            pl.BlockSpec((NB, tn2), lambda n: (0, n)),
            pl.BlockSpec((1, dc), lambda n: (0, 0)),
            pl.BlockSpec((1, dc), lambda n: (0, 0)),
            pl.BlockSpec((dc, tn2), lambda n: (0, n)),
            pl.BlockSpec((1, tn2), lambda n: (0, n)),
        ],
        out_specs=pl.BlockSpec((NB, tn2), lambda n: (0, n)),
        scratch_shapes=[pltpu.VMEM((NB, dc), BF16)],
        compiler_params=_params(1), name="conv_sample_b",
    )(c, x2d, row(lg), row(lb), w2, row(b2))
    return y, new_state


def _proj_kernel(x_ref, nw_ref, w_ref, o_ref, xn_ref, *, acts):
    s = pl.program_id(1)

    @pl.when(s == 0)
    def _():
        xn_ref[...] = _rms_bf16(x_ref[...], nw_ref[...])

    y = _dot(xn_ref[...], w_ref[0])
    silu_segs = [i for i, a in enumerate(acts) if a]
    is_silu = functools.reduce(jnp.logical_or, [s == i for i in silu_segs])

    @pl.when(is_silu)
    def _():
        o_ref[0] = _silu(y)

    @pl.when(jnp.logical_not(is_silu))
    def _():
        o_ref[0] = y


def _proj(x2d, nw, w_stack, acts, tm):
    T, D = x2d.shape
    S, _, N = w_stack.shape
    return pl.pallas_call(
        functools.partial(_proj_kernel, acts=acts),
        out_shape=jax.ShapeDtypeStruct((S, T, N), F32),
        grid=(T // tm, S),
        in_specs=[pl.BlockSpec((tm, D), lambda m, s: (m, 0)),
                  pl.BlockSpec((1, D), lambda m, s: (0, 0)),
                  pl.BlockSpec((1, D, N), lambda m, s: (s, 0, 0))],
        out_specs=pl.BlockSpec((1, tm, N), lambda m, s: (s, m, 0)),
        scratch_shapes=[pltpu.VMEM((tm, D), BF16)],
        compiler_params=_params(2), name="hgrn_proj",
    )(x2d, nw.reshape(1, D), w_stack)


def _residual_matmul_kernel(a_ref, w_ref, x_ref, o_ref):
    o_ref[...] = x_ref[...] + _dot(a_ref[...], w_ref[...])


def _residual_matmul(a, w, x2d, tm):
    T, K = a.shape
    N = w.shape[1]
    return pl.pallas_call(
        _residual_matmul_kernel,
        out_shape=jax.ShapeDtypeStruct((T, N), F32),
        grid=(T // tm,),
        in_specs=[pl.BlockSpec((tm, K), lambda m: (m, 0)),
                  _resident((K, N)),
                  pl.BlockSpec((tm, N), lambda m: (m, 0))],
        out_specs=pl.BlockSpec((tm, N), lambda m: (m, 0)),
        compiler_params=_params(1), name="hgrn_out",
    )(a, w, x2d)


def _forget_lower_bound(raw, layer):
    mx = jnp.max(raw, axis=0, keepdims=True)
    e = jnp.exp(raw - mx)
    sm = e / jnp.sum(e, axis=0, keepdims=True)
    first = sm[0:1]
    cum = first
    for i in range(1, layer + 1):
        cum = cum + sm[i:i + 1]
    return cum - first


def _log_forget_and_key(z, lb):
    log_sig = -(jnp.maximum(-z, 0.0) + jnp.log1p(jnp.exp(-jnp.abs(z))))
    a = jnp.log(lb)
    b = jnp.log1p(-lb) + log_sig
    log_f = jnp.maximum(a, b) + jnp.log1p(jnp.exp(-jnp.abs(a - b)))
    k = (1.0 - lb) * jax.nn.sigmoid(-z)
    return log_f, k


def _split3_bf16(x):
    hi = x.astype(BF16)
    r1 = x - hi.astype(F32)
    mid = r1.astype(BF16)
    lo = (r1 - mid.astype(F32)).astype(BF16)
    return hi, mid, lo


def _head_norm_gate(o, ng, gate):
    ms = jnp.mean(o * o, axis=-1, keepdims=True)
    return (o * lax.rsqrt(ms + EPS) * ng * gate).astype(BF16)


def _gla_prompt_kernel(q_ref, z_ref, v_ref, g_ref, lbr_ref, ng_ref, o_ref, s_ref, st_ref,
                       *, layer, seq_len, hg):
    C = min(GLA_CHUNK, seq_len)
    sub = min(GLA_SUB, C)
    n_sub = C // sub
    W = V7X_LANES
    st_ref[...] = jnp.zeros(st_ref.shape, F32)
    lb_all = _forget_lower_bound(lbr_ref[...], layer)
    tri = (lax.broadcasted_iota(jnp.int32, (C, C), 0)
           >= lax.broadcasted_iota(jnp.int32, (C, C), 1)).astype(BF16)
    row_c = lax.broadcasted_iota(jnp.int32, (C, W), 0)
    a_row = lax.broadcasted_iota(jnp.int32, (sub, C), 0)
    a_col = lax.broadcasted_iota(jnp.int32, (sub, C), 1)

    def chunk(ci, carry):
        r0 = pl.multiple_of(ci * C, C)
        rows = pl.ds(r0, C)
        for hh in range(hg):
            ls = slice(hh * W, (hh + 1) * W)
            q = q_ref[0, 0, rows, ls]
            z = z_ref[0, 0, rows, ls]
            v = v_ref[0, 0, rows, ls].astype(BF16)
            log_f, k = _log_forget_and_key(z, lb_all[:, ls])
            hi, mid, lo = _split3_bf16(log_f)
            G = _dot(tri, hi) + _dot(tri, mid) + _dot(tri, lo)
            g_last = G[C - 1:C, :]
            st = st_ref[hh]
            o = _dot_nt((q * jnp.exp(G)).astype(BF16), st.astype(BF16))
            a_blocks = []
            for i in range(n_sub):
                sl = slice(i * sub, (i + 1) * sub)
                Gi, qi, ki = G[sl], q[sl], k[sl]
                ad = jnp.zeros((sub, C), F32)
                for s in range(sub):
                    lo_row = (s // V7X_SUBLANES) * V7X_SUBLANES
                    e = (jnp.exp(jnp.minimum(Gi[lo_row:] - Gi[s:s + 1], 0.0))
                         * (qi[lo_row:] * ki[s:s + 1]))
                    col = jnp.sum(e, axis=-1, keepdims=True)
                    if lo_row:
                        col = jnp.concatenate([jnp.zeros((lo_row, 1), F32), col], axis=0)
                    ad = jnp.where(a_col == i * sub + s, col, ad)
                a_i = jnp.where(a_col <= a_row + i * sub, ad, 0.0)
                if i > 0:
                    g_ref_row = G[i * sub - 1:i * sub, :]
                    qt = (qi * jnp.exp(Gi - g_ref_row)).astype(BF16)
                    kt = jnp.where(row_c < i * sub,
                                   k * jnp.exp(jnp.minimum(g_ref_row - G, 0.0)), 0.0).astype(BF16)
                    a_i = a_i + _dot_nt(qt, kt)
                a_blocks.append(a_i)
            A = jnp.concatenate(a_blocks, axis=0) if n_sub > 1 else a_blocks[0]
            o = o + _dot(A.astype(BF16), v)
            kd = (k * jnp.exp(g_last - G)).astype(BF16)
            st_ref[hh] = jnp.exp(g_last) * st + _dot_tn(v, kd)
            o_ref[0, rows, ls] = _head_norm_gate(o, ng_ref[:, ls], g_ref[0, 0, rows, ls])
        return carry

    lax.fori_loop(0, seq_len // C, chunk, 0)
    for hh in range(hg):
        s_ref[0, hh] = st_ref[hh].T


def _gla_prompt(P, lb_raw, layer, ng, n_heads):
    _, B, L, D = P.shape
    W = D // n_heads
    assert W == V7X_LANES
    hg = 2
    NL = lb_raw.shape[0]
    seg = lambda s: pl.BlockSpec((1, 1, L, hg * W), lambda b, h: (s, b, 0, h))
    return pl.pallas_call(
        functools.partial(_gla_prompt_kernel, layer=layer, seq_len=L, hg=hg),
        out_shape=(jax.ShapeDtypeStruct((B, L, D), BF16),
                   jax.ShapeDtypeStruct((B, n_heads, W, W), F32)),
        grid=(B, n_heads // hg),
        in_specs=[seg(0), seg(1), seg(2), seg(3),
                  pl.BlockSpec((NL, hg * W), lambda b, h: (0, h)),
                  pl.BlockSpec((1, hg * W), lambda b, h: (0, h))],
        out_specs=(pl.BlockSpec((1, L, hg * W), lambda b, h: (b, 0, h)),
                   pl.BlockSpec((1, hg, W, W), lambda b, h: (b, h, 0, 0))),
        scratch_shapes=[pltpu.VMEM((hg, W, W), F32)],
        compiler_params=_params(2), name="gla_prompt",
    )(P, P, P, P, lb_raw, ng.reshape(1, D))


def _gla_sample_kernel(q_ref, z_ref, v_ref, g_ref, lbr_ref, ng_ref, s_ref, o_ref, ns_ref,
                       qt_ref, ft_ref, kt_ref, oacc_ref, *, layer, bb):
    j = pl.program_id(1)

    @pl.when(j == 0)
    def _():
        lb = _forget_lower_bound(lbr_ref[...], layer)
        log_f, k = _log_forget_and_key(z_ref[0], lb)
        qt_ref[...] = q_ref[0].T
        ft_ref[...] = jnp.exp(log_f).T
        kt_ref[...] = k.T

    shift = (V7X_LANES - j * bb) % V7X_LANES
    qt = pltpu.roll(qt_ref[...], shift, axis=1)
    ft = pltpu.roll(ft_ref[...], shift, axis=1)
    kt = pltpu.roll(kt_ref[...], shift, axis=1)
    r0 = pl.multiple_of(j * bb, bb)
    v_blk = v_ref[0, pl.ds(r0, bb), :]
    for i in range(bb):
        s_new = ft[:, i:i + 1] * s_ref[i, 0] + kt[:, i:i + 1] * v_blk[i:i + 1, :]
        ns_ref[i, 0] = s_new
        oacc_ref[i:i + 1, :] = jnp.sum(qt[:, i:i + 1] * s_new, axis=0, keepdims=True)
    o_ref[...] = _head_norm_gate(oacc_ref[...], ng_ref[...], g_ref[0, pl.ds(r0, bb), :])


def _gla_sample(P, state, lb_raw, layer, ng):
    _, NB, D = P.shape
    _, H, DK, DV = state.shape
    assert NB == V7X_LANES and DK == V7X_LANES and DV == V7X_LANES
    bb = 32
    NL = lb_raw.shape[0]
    seg = lambda s: pl.BlockSpec((1, NB, DK), lambda h, j: (s, 0, h))
    return pl.pallas_call(
        functools.partial(_gla_sample_kernel, layer=layer, bb=bb),
        out_shape=(jax.ShapeDtypeStruct((NB, D), BF16),
                   jax.ShapeDtypeStruct(state.shape, F32)),
        grid=(H, NB // bb),
        in_specs=[seg(0), seg(1), seg(2), seg(3),
                  pl.BlockSpec((NL, DK), lambda h, j: (0, h)),
                  pl.BlockSpec((1, DV), lambda h, j: (0, h)),
                  pl.BlockSpec((bb, 1, DK, DV), lambda h, j: (j, h, 0, 0))],
        out_specs=(pl.BlockSpec((bb, DV), lambda h, j: (j, h)),
                   pl.BlockSpec((bb, 1, DK, DV), lambda h, j: (j, h, 0, 0))),
        scratch_shapes=[pltpu.VMEM((DK, NB), F32)] * 3 + [pltpu.VMEM((bb, DV), F32)],
        compiler_params=_params(2), name="gla_sample",
    )(P, P, P, P, lb_raw, ng.reshape(1, D), state)


def kernel(x_prompt, x_sample, state_conv, state_hgrn, state_ffn, norm_mix, norm_ffn, norm_final,
           conv_w_pw1, conv_b_pw1, conv_w_dw, conv_b_dw, conv_ln_g, conv_ln_b, conv_w_pw2,
           conv_b_pw2, hgrn_w_q, hgrn_w_f, hgrn_w_i, hgrn_w_g, hgrn_w_o, hgrn_lb_raw,
           hgrn_norm_g, ffn_w_up, ffn_w_dw, ffn_b_dw, ffn_w_down):
    B, L, D = x_prompt.shape
    NB = x_sample.shape[0]
    depth = ffn_w_up.shape[0]
    n_heads = state_hgrn.shape[2]
    F = ffn_w_down.shape[1]
    lb_raw = hgrn_lb_raw.astype(F32)

    xp = x_prompt.reshape(B * L, D)
    xs = x_sample.reshape(NB, D)
    conv_p, hgrn_p, ffn_p, conv_s, hgrn_s, ffn_s = [], [], [], [], [], []
    acts = (True, False, False, True)
    for i in range(depth):
        j = i // 2
        if i % 2 == 0:
            w1 = conv_w_pw1[j].astype(BF16)
            w2 = conv_w_pw2[j].astype(BF16)
            cargs = (norm_mix[i], w1, conv_b_pw1[j], conv_w_dw[j], conv_b_dw[j],
                     conv_ln_g[j], conv_ln_b[j], w2, conv_b_pw2[j])
            xp, st = _conv_prompt(xp, B, L, *cargs)
            conv_p.append(st)
            xs, st = _conv_sample(xs, state_conv[j], *cargs)
            conv_s.append(st)
        else:
            w_stack = jnp.stack([hgrn_w_q[j], hgrn_w_f[j], hgrn_w_i[j], hgrn_w_g[j]]).astype(BF16)
            w_o = hgrn_w_o[j].astype(BF16)
            P = _proj(xp, norm_mix[i], w_stack, acts, tm=min(PROJ_TM, L))
            og, st = _gla_prompt(P.reshape(4, B, L, D), lb_raw, j, hgrn_norm_g[j], n_heads)
            hgrn_p.append(st)
            xp = _residual_matmul(og.reshape(B * L, D), w_o, xp, tm=min(PROJ_TM, L))
            P = _proj(xs, norm_mix[i], w_stack, acts, tm=NB)
            og, st = _gla_sample(P, state_hgrn[j], lb_raw, j, hgrn_norm_g[j])
            hgrn_s.append(st)
            xs = _residual_matmul(og, w_o, xs, tm=NB)
        w_up = ffn_w_up[i].astype(BF16)
        w_down = ffn_w_down[i].astype(BF16)
        last = i == depth - 1
        fargs = (norm_ffn[i], w_up, ffn_w_dw[i], ffn_b_dw[i], w_down, norm_final)
        xp, st = _ffn(xp, L, *fargs, final_norm=last)
        ffn_p.append(st.reshape(B, 2, 2 * F))
        prev = state_ffn[i].reshape(NB, 2, 2, F).transpose(1, 2, 0, 3)
        xs, hn = _ffn(xs, 1, *fargs, prev=prev, final_norm=last)
        new_prev = jnp.stack([prev[1], hn], axis=0)
        ffn_s.append(new_prev.transpose(2, 0, 1, 3).reshape(NB, 2, 2 * F))
    return (xp.reshape(B, L, D), xs.reshape(NB, 1, D),
            jnp.stack(conv_p), jnp.stack(hgrn_p), jnp.stack(ffn_p),
            jnp.stack(conv_s), jnp.stack(hgrn_s), jnp.stack(ffn_s))
```

```python
import functools

import jax
import jax.numpy as jnp
from jax import lax
from jax.experimental import pallas as pl
from jax.experimental.pallas import tpu as pltpu

F32 = jnp.float32
BF16 = jnp.bfloat16
EPS = 1e-6
LOG2E = 1.4426950408889634

V7X_VMEM_BYTES = 64 * 1024 * 1024
V7X_LANES = 128
V7X_SUBLANES = 8
V7X_MXU_DIM = 256
VMEM_LIMIT = V7X_VMEM_BYTES - 8 * 1024 * 1024

GLA_CHUNK = 64
GLA_SUB = 16
GLA_HEADS_PER_STEP = 8
GLA_TL = 512
FFN_TM = 512
FFN_TF = 512
CONV_TL = 256
PROJ_TM = 512


def _params(n_axes, vmem=VMEM_LIMIT):
    return pltpu.CompilerParams(
        dimension_semantics=("arbitrary",) * n_axes, vmem_limit_bytes=vmem)


def _resident_layer(shape, layer):
    n = len(shape)
    return pl.BlockSpec((None,) + tuple(shape), lambda *_: (layer,) + (0,) * n,
                        pipeline_mode=pl.Buffered(1))


def _rms_bf16(x, w):
    ms = jnp.mean(x * x, axis=-1, keepdims=True)
    return (x * lax.rsqrt(ms + EPS) * w).astype(BF16)


def _silu(x):
    return x * jax.nn.sigmoid(x)


def _dot(a, b):
    return jnp.dot(a, b, preferred_element_type=F32)


def _dot_nt(a, b):
    return lax.dot_general(a, b, (((1,), (1,)), ((), ())), preferred_element_type=F32)


def _dot_tn(a, b):
    return lax.dot_general(a, b, (((0,), (0,)), ((), ())), preferred_element_type=F32)


def _ffn_kernel(*refs, prompt, tm, tiles_per_seq, final_norm, nf, n_sub):
    if prompt:
        (x_ref, nw_ref, wg_ref, wu_ref, cw_ref, cb_ref, wd_ref, fw_ref,
         o_ref, st_ref, xn_ref, hs_ref, carry_ref) = refs
    else:
        (x_ref, nw_ref, wg_ref, wu_ref, cw_ref, cb_ref, wd_ref, fw_ref, prev_ref,
         o_ref, hn_ref, xn_ref) = refs
    m = pl.program_id(0)
    f = pl.program_id(1)

    @pl.when(f == 0)
    def _():
        x = x_ref[...]
        xn_ref[...] = _rms_bf16(x, nw_ref[...])
        o_ref[...] = x

    xn = xn_ref[...]
    tf = wg_ref.shape[1]
    sw = tf // n_sub
    acc = None
    hs_all = [[_dot(xn, w_ref[:, c * sw:(c + 1) * sw]) for w_ref in (wg_ref, wu_ref)]
              for c in range(n_sub)]
    for c in range(n_sub):
        cs = slice(c * sw, (c + 1) * sw)
        ys = []
        for half in (0, 1):
            h = hs_all[c][half]
            if prompt:
                seq_start = (m % tiles_per_seq) == 0
                prev = carry_ref[f, half, :, cs]
                hs_ref[half, 0:V7X_SUBLANES, cs] = jnp.where(seq_start, 0.0, prev)
                hs_ref[half, V7X_SUBLANES:, cs] = h
                carry_ref[f, half, :, cs] = h[tm - V7X_SUBLANES:, :]
                cols = pl.ds(pl.multiple_of(f * tf + c * sw, sw), sw)
                for r in range(2):
                    st_ref[0, r, half:half + 1, cols] = h[tm - 2 + r:tm - 1 + r, :]
                h1 = hs_ref[half, V7X_SUBLANES - 1:V7X_SUBLANES - 1 + tm, cs]
                h2 = hs_ref[half, V7X_SUBLANES - 2:V7X_SUBLANES - 2 + tm, cs]
            else:
                hn_ref[half, :, cs] = h
                h2 = prev_ref[0, half, :, cs]
                h1 = prev_ref[1, half, :, cs]
            y = (cw_ref[2, half:half + 1, cs] * h + cw_ref[1, half:half + 1, cs] * h1
                 + cw_ref[0, half:half + 1, cs] * h2 + cb_ref[half:half + 1, cs])
            ys.append(y)
        p = (_silu(ys[0]) * ys[1]).astype(BF16)
        d = _dot(p, wd_ref[cs, :])
        acc = d if acc is None else acc + d
    o_ref[...] += acc

    if final_norm:
        @pl.when(f == nf - 1)
        def _():
            o = o_ref[...]
            ms = jnp.mean(o * o, axis=-1, keepdims=True)
            o_ref[...] = o * lax.rsqrt(ms + EPS) * fw_ref[...]


def _ffn(x2d, seq_len, layer, nw, w_up, w_dw, b_dw, w_down, fw, prev=None, final_norm=False):
    T, D = x2d.shape
    F = w_down.shape[1]
    prompt = prev is None
    tm = min(FFN_TM, seq_len) if prompt else T
    tf = FFN_TF if F % FFN_TF == 0 else F
    nf = F // tf
    nm = T // tm
    n_sub = 1
    cw = w_dw.reshape(3, 2, F)
    cb = b_dw.reshape(2, F)
    in_specs = [
        pl.BlockSpec((tm, D), lambda m, f: (m, 0)),
        pl.BlockSpec((1, D), lambda m, f: (0, 0)),
        pl.BlockSpec((None, D, tf), lambda m, f: (layer, 0, f)),
        pl.BlockSpec((None, D, tf), lambda m, f: (layer, 0, nf + f)),
        pl.BlockSpec((3, 2, tf), lambda m, f: (0, 0, f)),
        pl.BlockSpec((2, tf), lambda m, f: (0, f)),
        pl.BlockSpec((None, tf, D), lambda m, f: (layer, f, 0)),
        pl.BlockSpec((1, D), lambda m, f: (0, 0)),
    ]
    args = [x2d, nw.reshape(1, D), w_up, w_up, cw, cb, w_down, fw.reshape(1, D)]
    scratch = [pltpu.VMEM((tm, D), BF16)]
    if prompt:
        tiles_per_seq = seq_len // tm
        nb = T // seq_len
        out_shape = (jax.ShapeDtypeStruct((T, D), F32),
                     jax.ShapeDtypeStruct((nb, 2, 2, F), F32))
        out_specs = (pl.BlockSpec((tm, D), lambda m, f: (m, 0)),
                     pl.BlockSpec((1, 2, 2, F), lambda m, f: (m // tiles_per_seq, 0, 0, 0)))
        scratch += [pltpu.VMEM((2, tm + V7X_SUBLANES, tf), F32),
                    pltpu.VMEM((nf, 2, V7X_SUBLANES, tf), F32)]
    else:
        tiles_per_seq = 1
        in_specs.append(pl.BlockSpec((2, 2, tm, tf), lambda m, f: (0, 0, m, f)))
        args.append(prev)
        out_shape = (jax.ShapeDtypeStruct((T, D), F32),
                     jax.ShapeDtypeStruct((2, T, F), F32))
        out_specs = (pl.BlockSpec((tm, D), lambda m, f: (m, 0)),
                     pl.BlockSpec((2, tm, tf), lambda m, f: (0, m, f)))
    kern = functools.partial(_ffn_kernel, prompt=prompt, tm=tm, tiles_per_seq=tiles_per_seq,
                             final_norm=final_norm, nf=nf, n_sub=n_sub)
    return pl.pallas_call(
        kern, out_shape=out_shape, grid=(nm, nf), in_specs=in_specs, out_specs=out_specs,
        scratch_shapes=scratch, compiler_params=_params(2),
        name="ffn_prompt" if prompt else "ffn_sample")(*args)


def _ln_silu_bf16(c, g, b):
    mu = jnp.mean(c, axis=-1, keepdims=True)
    d = c - mu
    var = jnp.mean(d * d, axis=-1, keepdims=True)
    return _silu(d * lax.rsqrt(var + EPS) * g + b).astype(BF16)


def _conv_prompt_kernel(x_ref, nw_ref, w1_ref, b1_ref, cw_ref, cb_ref, lg_ref, lb_ref,
                        w2_ref, b2_ref, o_ref, st_ref, us_ref, c_ref, cwb_ref,
                        *, tl, nt, dc, width):
    t = pl.program_id(1)
    S = V7X_SUBLANES
    halo = 32
    lead = halo - (width - 1)
    x = x_ref[...]
    xn = _rms_bf16(x, nw_ref[...])
    h = _dot(xn, w1_ref[...]) + b1_ref[...]
    u = h[:, :dc] * jax.nn.sigmoid(h[:, dc:])

    @pl.when(t == 0)
    def _():
        us_ref[0:halo, :] = jnp.zeros((halo, dc), F32)
        for j in range(width):
            cwb_ref[j] = jnp.broadcast_to(cw_ref[j:j + 1, :], (S, dc))

    us_ref[halo:, :] = u

    rb = min(128, tl)
    cwid = V7X_LANES

    def col_body(ci, carry):
        c0 = pl.multiple_of(ci * cwid, cwid)
        cols = pl.ds(c0, cwid)
        for r0 in range(0, tl, rb):
            acc = jnp.broadcast_to(cb_ref[:, cols], (rb // S, S, cwid))
            for a in range(S):
                offs = [lead + j for j in range(width) if (lead + j) % S == a]
                if not offs:
                    continue
                start = offs[0] - a
                span = -(-(rb + offs[-1] - start) // S) * S
                blk = us_ref[pl.ds(r0 + start, span), cols]
                if a:
                    blk = pltpu.roll(blk, span - a, axis=0)
                for off in offs:
                    d = off - offs[0]
                    acc = acc + cwb_ref[off - lead, :, cols] * blk[d:d + rb].reshape(rb // S, S, cwid)
            c_ref[pl.ds(r0, rb), cols] = acc.reshape(rb, cwid)
        return carry

    lax.fori_loop(0, dc // cwid, col_body, 0)

    @pl.when(t == nt - 1)
    def _():
        st_ref[0] = us_ref[halo + tl - (width - 1):halo + tl, :]

    us_ref[0:halo, :] = u[tl - halo:, :]

    c = _ln_silu_bf16(c_ref[...], lg_ref[...], lb_ref[...])
    o_ref[...] = x + _dot(c, w2_ref[...]) + b2_ref[...]


def _conv_prompt(x2d, nb, seq_len, layer, nw, w1, b1, cw, cb, lg, lb, w2, b2):
    T, D = x2d.shape
    dc = w2.shape[1]
    width = cw.shape[0]
    tl = min(CONV_TL, seq_len)
    nt = seq_len // tl
    row = lambda v: v.reshape(1, -1)
    kern = functools.partial(_conv_prompt_kernel, tl=tl, nt=nt, dc=dc, width=width)
    const2 = lambda shape: pl.BlockSpec(shape, lambda b, t: (0, 0))
    return pl.pallas_call(
        kern,
        out_shape=(jax.ShapeDtypeStruct((T, D), F32),
                   jax.ShapeDtypeStruct((nb, width - 1, dc), F32)),
        grid=(nb, nt),
        in_specs=[
            pl.BlockSpec((tl, D), lambda b, t: (b * nt + t, 0)),
            const2((1, D)),
            _resident_layer((D, 2 * dc), layer),
            const2((1, 2 * dc)),
            const2((width, dc)),
            const2((1, dc)), const2((1, dc)), const2((1, dc)),
            _resident_layer((dc, D), layer),
            const2((1, D)),
        ],
        out_specs=(pl.BlockSpec((tl, D), lambda b, t: (b * nt + t, 0)),
                   pl.BlockSpec((1, width - 1, dc), lambda b, t: (b, 0, 0))),
        scratch_shapes=[pltpu.VMEM((32 + tl, dc), F32), pltpu.VMEM((tl, dc), F32),
                        pltpu.VMEM((width, V7X_SUBLANES, dc), F32)],
        compiler_params=_params(2), name="conv_prompt",
    )(x2d, row(nw), w1, row(b1), cw, row(cb), row(lg), row(lb), w2, row(b2))


def _conv_sample_a_kernel(x_ref, nw_ref, wa_ref, wg_ref, ba_ref, bg_ref, cw_ref, cb_ref,
                          st_ref, c_ref, ns_ref, xn_ref, *, width):
    n = pl.program_id(0)

    @pl.when(n == 0)
    def _():
        xn_ref[...] = _rms_bf16(x_ref[...], nw_ref[...])

    xn = xn_ref[...]
    u = (_dot(xn, wa_ref[...]) + ba_ref[...]) * jax.nn.sigmoid(_dot(xn, wg_ref[...]) + bg_ref[...])
    acc = cb_ref[...] + cw_ref[width - 1:width, :] * u
    for j in range(width - 1):
        row = st_ref[:, j, :]
        acc = acc + cw_ref[j:j + 1, :] * row
        if j > 0:
            ns_ref[:, j - 1, :] = row
    ns_ref[:, width - 2, :] = u
    c_ref[...] = acc


def _conv_sample_b_kernel(c_ref, x_ref, lg_ref, lb_ref, w2_ref, b2_ref, o_ref, cn_ref):
    n = pl.program_id(0)

    @pl.when(n == 0)
    def _():
        cn_ref[...] = _ln_silu_bf16(c_ref[...], lg_ref[...], lb_ref[...])

    o_ref[...] = x_ref[...] + _dot(cn_ref[...], w2_ref[...]) + b2_ref[...]


def _conv_sample(x2d, state, layer, nw, w1, b1, cw, cb, lg, lb, w2, b2):
    NB, D = x2d.shape
    dc = w2.shape[1]
    width = cw.shape[0]
    tn = min(256, dc)
    nn = dc // tn
    row = lambda v: v.reshape(1, -1)
    c, new_state = pl.pallas_call(
        functools.partial(_conv_sample_a_kernel, width=width),
        out_shape=(jax.ShapeDtypeStruct((NB, dc), F32),
                   jax.ShapeDtypeStruct((NB, width - 1, dc), F32)),
        grid=(nn,),
        in_specs=[
            pl.BlockSpec((NB, D), lambda n: (0, 0)),
            pl.BlockSpec((1, D), lambda n: (0, 0)),
            pl.BlockSpec((None, D, tn), lambda n: (layer, 0, n)),
            pl.BlockSpec((None, D, tn), lambda n: (layer, 0, nn + n)),
            pl.BlockSpec((1, tn), lambda n: (0, n)),
            pl.BlockSpec((1, tn), lambda n: (0, nn + n)),
            pl.BlockSpec((width, tn), lambda n: (0, n)),
            pl.BlockSpec((1, tn), lambda n: (0, n)),
            pl.BlockSpec((None, NB, width - 1, tn), lambda n: (layer, 0, 0, n)),
        ],
        out_specs=(pl.BlockSpec((NB, tn), lambda n: (0, n)),
                   pl.BlockSpec((NB, width - 1, tn), lambda n: (0, 0, n))),
        scratch_shapes=[pltpu.VMEM((NB, D), BF16)],
        compiler_params=_params(1), name="conv_sample_a",
    )(x2d, row(nw), w1, w1, row(b1), row(b1), cw, row(cb), state)
    tn2 = min(512, D)
    y = pl.pallas_call(
        _conv_sample_b_kernel,
        out_shape=jax.ShapeDtypeStruct((NB, D), F32),
        grid=(D // tn2,),
        in_specs=[
            pl.BlockSpec((NB, dc), lambda n: (0, 0)),
            pl.BlockSpec((NB, tn2), lambda n: (0, n)),
            pl.BlockSpec((1, dc), lambda n: (0, 0)),
            pl.BlockSpec((1, dc), lambda n: (0, 0)),
            pl.BlockSpec((None, dc, tn2), lambda n: (layer, 0, n)),
            pl.BlockSpec((1, tn2), lambda n: (0, n)),
        ],
        out_specs=pl.BlockSpec((NB, tn2), lambda n: (0, n)),
        scratch_shapes=[pltpu.VMEM((NB, dc), BF16)],
        compiler_params=_params(1), name="conv_sample_b",
    )(c, x2d, row(lg), row(lb), w2, row(b2))
    return y, new_state


def _proj_kernel(x_ref, nw_ref, w_ref, o_ref, xn_ref, *, acts):
    s = pl.program_id(1)

    @pl.when(s == 0)
    def _():
        xn_ref[...] = _rms_bf16(x_ref[...], nw_ref[...])

    silu_segs = [i for i, a in enumerate(acts) if a]
    is_silu = functools.reduce(jnp.logical_or, [s == i for i in silu_segs])
    n = w_ref.shape[2]
    cw = 512 if n % 512 == 0 else n

    def segment(act):
        xn = xn_ref[...]
        for c0 in range(0, n, cw):
            y = _dot(xn, w_ref[0, :, c0:c0 + cw])
            o_ref[0, :, c0:c0 + cw] = _silu(y) if act else y

    @pl.when(is_silu)
    def _():
        segment(True)

    @pl.when(jnp.logical_not(is_silu))
    def _():
        segment(False)


def _proj(x2d, nw, w_stack, acts, tm):
    T, D = x2d.shape
    S, _, N = w_stack.shape
    return pl.pallas_call(
        functools.partial(_proj_kernel, acts=acts),
        out_shape=jax.ShapeDtypeStruct((S, T, N), F32),
        grid=(T // tm, S),
        in_specs=[pl.BlockSpec((tm, D), lambda m, s: (m, 0)),
                  pl.BlockSpec((1, D), lambda m, s: (0, 0)),
                  pl.BlockSpec((1, D, N), lambda m, s: (s, 0, 0))],
        out_specs=pl.BlockSpec((1, tm, N), lambda m, s: (s, m, 0)),
        scratch_shapes=[pltpu.VMEM((tm, D), BF16)],
        compiler_params=_params(2), name="hgrn_proj",
    )(x2d, nw.reshape(1, D), w_stack)


def _residual_matmul_kernel(a_ref, w_ref, x_ref, o_ref):
    o_ref[...] = x_ref[...] + _dot(a_ref[...], w_ref[...])


def _residual_matmul(a, w, layer, x2d, tm):
    T, K = a.shape
    N = w.shape[2]
    return pl.pallas_call(
        _residual_matmul_kernel,
        out_shape=jax.ShapeDtypeStruct((T, N), F32),
        grid=(T // tm,),
        in_specs=[pl.BlockSpec((tm, K), lambda m: (m, 0)),
                  _resident_layer((K, N), layer),
                  pl.BlockSpec((tm, N), lambda m: (m, 0))],
        out_specs=pl.BlockSpec((tm, N), lambda m: (m, 0)),
        compiler_params=_params(1), name="hgrn_out",
    )(a, w, x2d)


def _forget_lower_bound(raw, layer):
    mx = jnp.max(raw, axis=0, keepdims=True)
    e = jnp.exp(raw - mx)
    sm = e / jnp.sum(e, axis=0, keepdims=True)
    first = sm[0:1]
    cum = first
    for i in range(1, layer + 1):
        cum = cum + sm[i:i + 1]
    return cum - first


def _log_forget_and_key(z, lb):
    t = jnp.exp(-jnp.abs(z))
    r = 1.0 / (1.0 + t)
    sig_neg = jnp.where(z >= 0.0, t * r, r)
    log_sig = jnp.minimum(z, 0.0) - jnp.log(1.0 + t)
    a = jnp.log(lb)
    b = jnp.log1p(-lb) + log_sig
    log_f = jnp.maximum(a, b) + jnp.log(1.0 + jnp.exp(-jnp.abs(a - b)))
    k = (1.0 - lb) * sig_neg
    return log_f, k


def _split3_bf16(x):
    hi = x.astype(BF16)
    r1 = x - hi.astype(F32)
    mid = r1.astype(BF16)
    lo = (r1 - mid.astype(F32)).astype(BF16)
    return hi, mid, lo


def _head_norm_gate(o, ng, gate):
    ms = jnp.mean(o * o, axis=-1, keepdims=True)
    return (o * lax.rsqrt(ms + EPS) * ng * gate).astype(BF16)


def _gla_prompt_kernel(q_ref, z_ref, v_ref, g_ref, lbr_ref, ng_ref, o_ref, s_ref,
                       st_ref, gh_ref, oacc_ref, x_ref, w_ref, ad_ref, *, layer, tl, nt, hg):
    C = min(GLA_CHUNK, tl)
    sub = min(GLA_SUB, C)
    n_sub = C // sub
    W = V7X_LANES
    t_idx = pl.program_id(2)

    @pl.when(t_idx == 0)
    def _():
        st_ref[...] = jnp.zeros(st_ref.shape, F32)
        r = lax.broadcasted_iota(jnp.int32, (sub * W, W), 0) // W
        c = lax.broadcasted_iota(jnp.int32, (sub * W, W), 1)
        for i in range(n_sub):
            w_ref[i] = (c == r + i * sub).astype(BF16)

    lb_all = _forget_lower_bound(lbr_ref[...], layer)
    tri = (lax.broadcasted_iota(jnp.int32, (C, C), 0)
           >= lax.broadcasted_iota(jnp.int32, (C, C), 1)).astype(BF16)
    a_row = lax.broadcasted_iota(jnp.int32, (sub, C), 0)
    a_col = lax.broadcasted_iota(jnp.int32, (sub, C), 1)
    zero_rows = jnp.zeros((V7X_SUBLANES, W), F32)

    def chunk(ci, carry):
        r0 = pl.multiple_of(ci * C, C)
        rows = pl.ds(r0, C)
        for hh in range(hg):
            ls = slice(hh * W, (hh + 1) * W)
            q = q_ref[0, 0, rows, ls]
            z = z_ref[0, 0, rows, ls]
            v = v_ref[0, 0, rows, ls].astype(BF16)
            log_f, k = _log_forget_and_key(z, lb_all[:, ls])
            hi, mid, lo = _split3_bf16(log_f)
            G = (_dot(tri, hi) + _dot(tri, mid) + _dot(tri, lo)) * LOG2E
            gh_ref[hh, 0] = G
            gh_ref[hh, 1] = G - jnp.log2(k)
            g_last = G[C - 1:C, :]
            st = st_ref[hh]
            o = _dot_nt((q * jnp.exp2(G)).astype(BF16), st.astype(BF16))
            off_blocks = [jnp.zeros((sub, C), BF16)]
            for i in range(n_sub):
                base = i * sub
                Gi, qi = G[base:base + sub], q[base:base + sub]
                for s in range(sub):
                    lo_row = (s // V7X_SUBLANES) * V7X_SUBLANES
                    h_s = gh_ref[hh, 1, base + s:base + s + 1, :]
                    e = jnp.exp2(jnp.minimum(Gi[lo_row:] - h_s, 0.0)) * qi[lo_row:]
                    if lo_row:
                        e = jnp.concatenate([zero_rows] * (lo_row // V7X_SUBLANES) + [e], axis=0)
                    x_ref[i, hh * sub:(hh + 1) * sub, s * W:(s + 1) * W] = e.astype(BF16)
                if i > 0:
                    g_prev = gh_ref[hh, 0, base - 1:base, :]
                    qt = (qi * jnp.exp2(Gi - g_prev)).astype(BF16)
                    kt = jnp.concatenate(
                        [(k[:base] * jnp.exp2(g_prev - G[:base])).astype(BF16),
                         jnp.zeros((C - base, W), BF16)], axis=0)
                    off_blocks.append(_dot_nt(qt, kt).astype(BF16))
            a_off = jnp.concatenate(off_blocks, axis=0) if n_sub > 1 else off_blocks[0]
            oacc_ref[hh] = o + _dot(a_off, v)
            kd = (k * jnp.exp2(g_last - G)).astype(BF16)
            st_ref[hh] = jnp.exp2(g_last) * st + _dot_tn(v, kd)
        for i in range(n_sub):
            ad_ref[i] = _dot(x_ref[i], w_ref[i])
        for hh in range(hg):
            ls = slice(hh * W, (hh + 1) * W)
            blocks = [jnp.where(a_col <= a_row + i * sub,
                                ad_ref[i, hh * sub:(hh + 1) * sub, 0:C], 0.0).astype(BF16)
                      for i in range(n_sub)]
            a_diag = jnp.concatenate(blocks, axis=0) if n_sub > 1 else blocks[0]
            o = oacc_ref[hh] + _dot(a_diag, v_ref[0, 0, rows, ls].astype(BF16))
            o_ref[0, rows, ls] = _head_norm_gate(o, ng_ref[:, ls], g_ref[0, 0, rows, ls])
        return carry

    lax.fori_loop(0, tl // C, chunk, 0)

    @pl.when(t_idx == nt - 1)
    def _():
        for hh in range(hg):
            s_ref[0, hh] = st_ref[hh].T


def _gla_prompt(P, lb_raw, layer, ng, n_heads):
    _, B, L, D = P.shape
    W = D // n_heads
    assert W == V7X_LANES
    hg = min(GLA_HEADS_PER_STEP, n_heads)
    tl = min(GLA_TL, L)
    nt = L // tl
    C = min(GLA_CHUNK, tl)
    sub = min(GLA_SUB, C)
    n_sub = C // sub
    NL = lb_raw.shape[0]
    seg = lambda s: pl.BlockSpec((1, 1, tl, hg * W), lambda b, h, t: (s, b, t, h))
    return pl.pallas_call(
        functools.partial(_gla_prompt_kernel, layer=layer, tl=tl, nt=nt, hg=hg),
        out_shape=(jax.ShapeDtypeStruct((B, L, D), BF16),
                   jax.ShapeDtypeStruct((B, n_heads, W, W), F32)),
        grid=(B, n_heads // hg, nt),
        in_specs=[seg(0), seg(1), seg(2), seg(3),
                  pl.BlockSpec((NL, hg * W), lambda b, h, t: (0, h)),
                  pl.BlockSpec((1, hg * W), lambda b, h, t: (0, h))],
        out_specs=(pl.BlockSpec((1, tl, hg * W), lambda b, h, t: (b, t, h)),
                   pl.BlockSpec((1, hg, W, W), lambda b, h, t: (b, h, 0, 0))),
        scratch_shapes=[pltpu.VMEM((hg, W, W), F32),
                        pltpu.VMEM((hg, 2, C, W), F32),
                        pltpu.VMEM((hg, C, W), F32),
                        pltpu.VMEM((n_sub, hg * sub, sub * W), BF16),
                        pltpu.VMEM((n_sub, sub * W, W), BF16),
                        pltpu.VMEM((n_sub, hg * sub, W), F32)],
        compiler_params=_params(3), name="gla_prompt",
    )(P, P, P, P, lb_raw, ng.reshape(1, D))


def _gla_sample_kernel(q_ref, z_ref, v_ref, g_ref, lbr_ref, ng_ref, s_ref, o_ref, ns_ref,
                       qt_ref, ft_ref, kt_ref, oacc_ref, *, layer, bb):
    j = pl.program_id(1)

    @pl.when(j == 0)
    def _():
        lb = _forget_lower_bound(lbr_ref[...], layer)
        log_f, k = _log_forget_and_key(z_ref[0], lb)
        qt_ref[...] = q_ref[0].T
        ft_ref[...] = jnp.exp(log_f).T
        kt_ref[...] = k.T

    shift = (V7X_LANES - j * bb) % V7X_LANES
    qt = pltpu.roll(qt_ref[...], shift, axis=1)
    ft = pltpu.roll(ft_ref[...], shift, axis=1)
    kt = pltpu.roll(kt_ref[...], shift, axis=1)
    r0 = pl.multiple_of(j * bb, bb)
    v_blk = v_ref[0, pl.ds(r0, bb), :]
    for i in range(bb):
        s_new = ft[:, i:i + 1] * s_ref[i, 0] + kt[:, i:i + 1] * v_blk[i:i + 1, :]
        ns_ref[i, 0] = s_new
        oacc_ref[i:i + 1, :] = jnp.sum(qt[:, i:i + 1] * s_new, axis=0, keepdims=True)
    o_ref[...] = _head_norm_gate(oacc_ref[...], ng_ref[...], g_ref[0, pl.ds(r0, bb), :])


def _gla_sample(P, state, lb_raw, layer, ng):
    _, NB, D = P.shape
    _, _, H, DK, DV = state.shape
    assert NB == V7X_LANES and DK == V7X_LANES and DV == V7X_LANES
    bb = 32
    NL = lb_raw.shape[0]
    seg = lambda s: pl.BlockSpec((1, NB, DK), lambda h, j: (s, 0, h))
    return pl.pallas_call(
        functools.partial(_gla_sample_kernel, layer=layer, bb=bb),
        out_shape=(jax.ShapeDtypeStruct((NB, D), BF16),
                   jax.ShapeDtypeStruct(state.shape[1:], F32)),
        grid=(H, NB // bb),
        in_specs=[seg(0), seg(1), seg(2), seg(3),
                  pl.BlockSpec((NL, DK), lambda h, j: (0, h)),
                  pl.BlockSpec((1, DV), lambda h, j: (0, h)),
                  pl.BlockSpec((None, bb, 1, DK, DV), lambda h, j: (layer, j, h, 0, 0))],
        out_specs=(pl.BlockSpec((bb, DV), lambda h, j: (j, h)),
                   pl.BlockSpec((bb, 1, DK, DV), lambda h, j: (j, h, 0, 0))),
        scratch_shapes=[pltpu.VMEM((DK, NB), F32)] * 3 + [pltpu.VMEM((bb, DV), F32)],
        compiler_params=_params(2), name="gla_sample",
    )(P, P, P, P, lb_raw, ng.reshape(1, D), state)


def kernel(x_prompt, x_sample, state_conv, state_hgrn, state_ffn, norm_mix, norm_ffn, norm_final,
           conv_w_pw1, conv_b_pw1, conv_w_dw, conv_b_dw, conv_ln_g, conv_ln_b, conv_w_pw2,
           conv_b_pw2, hgrn_w_q, hgrn_w_f, hgrn_w_i, hgrn_w_g, hgrn_w_o, hgrn_lb_raw,
           hgrn_norm_g, ffn_w_up, ffn_w_dw, ffn_b_dw, ffn_w_down):
    B, L, D = x_prompt.shape
    NB = x_sample.shape[0]
    depth = ffn_w_up.shape[0]
    n_heads = state_hgrn.shape[2]
    F = ffn_w_down.shape[1]
    lb_raw = hgrn_lb_raw.astype(F32)

    xp = x_prompt.reshape(B * L, D)
    xs = x_sample.reshape(NB, D)
    conv_p, hgrn_p, ffn_p, conv_s, hgrn_s, ffn_s = [], [], [], [], [], []
    acts = (True, False, False, True)
    w_pw1 = conv_w_pw1.astype(BF16)
    w_pw2 = conv_w_pw2.astype(BF16)
    w_o = hgrn_w_o.astype(BF16)
    w_up = ffn_w_up.astype(BF16)
    w_down = ffn_w_down.astype(BF16)
    prev_all = state_ffn.reshape(depth, NB, 2, 2, F).transpose(0, 2, 3, 1, 4)
    for i in range(depth):
        j = i // 2
        if i % 2 == 0:
            cargs = (j, norm_mix[i], w_pw1, conv_b_pw1[j], conv_w_dw[j], conv_b_dw[j],
                     conv_ln_g[j], conv_ln_b[j], w_pw2, conv_b_pw2[j])
            xp, st = _conv_prompt(xp, B, L, *cargs)
            conv_p.append(st)
            xs, st = _conv_sample(xs, state_conv, *cargs)
            conv_s.append(st)
        else:
            w_stack = jnp.stack([hgrn_w_q[j], hgrn_w_f[j], hgrn_w_i[j], hgrn_w_g[j]]).astype(BF16)
            P = _proj(xp, norm_mix[i], w_stack, acts, tm=min(PROJ_TM, L))
            og, st = _gla_prompt(P.reshape(4, B, L, D), lb_raw, j, hgrn_norm_g[j], n_heads)
            hgrn_p.append(st)
            xp = _residual_matmul(og.reshape(B * L, D), w_o, j, xp, tm=min(PROJ_TM, L))
            P = _proj(xs, norm_mix[i], w_stack, acts, tm=NB)
            og, st = _gla_sample(P, state_hgrn, lb_raw, j, hgrn_norm_g[j])
            hgrn_s.append(st)
            xs = _residual_matmul(og, w_o, j, xs, tm=NB)
        last = i == depth - 1
        fargs = (i, norm_ffn[i], w_up, ffn_w_dw[i], ffn_b_dw[i], w_down, norm_final)
        xp, st = _ffn(xp, L, *fargs, final_norm=last)
        ffn_p.append(st.reshape(B, 2, 2 * F))
        prev = prev_all[i]
        xs, hn = _ffn(xs, 1, *fargs, prev=prev, final_norm=last)
        new_prev = jnp.stack([prev[1], hn], axis=0)
        ffn_s.append(new_prev.transpose(2, 0, 1, 3).reshape(NB, 2, 2 * F))
    return (xp.reshape(B, L, D), xs.reshape(NB, 1, D),
            jnp.stack(conv_p), jnp.stack(hgrn_p), jnp.stack(ffn_p),
            jnp.stack(conv_s), jnp.stack(hgrn_s), jnp.stack(ffn_s))
```

```python
import functools

import jax
import jax.numpy as jnp
from jax import lax
from jax.experimental import pallas as pl
from jax.experimental.pallas import tpu as pltpu

F32 = jnp.float32
BF16 = jnp.bfloat16
EPS = 1e-6
LOG2E = 1.4426950408889634

V7X_VMEM_BYTES = 64 * 1024 * 1024
V7X_LANES = 128
V7X_SUBLANES = 8
V7X_MXU_DIM = 256
VMEM_LIMIT = V7X_VMEM_BYTES - 8 * 1024 * 1024

GLA_CHUNK = 64
GLA_SUB = 16
GLA_HEADS_PER_STEP = 8
GLA_TL = 1024
FFN_TM = 512
FFN_TF = 512
CONV_TL = 256
PROJ_TM = 512


def _params(n_axes, vmem=VMEM_LIMIT):
    return pltpu.CompilerParams(
        dimension_semantics=("arbitrary",) * n_axes, vmem_limit_bytes=vmem)


def _resident_layer(shape, layer):
    n = len(shape)
    return pl.BlockSpec((None,) + tuple(shape), lambda *_: (layer,) + (0,) * n,
                        pipeline_mode=pl.Buffered(1))


def _rms_bf16(x, w):
    ms = jnp.mean(x * x, axis=-1, keepdims=True)
    return (x * lax.rsqrt(ms + EPS) * w).astype(BF16)


def _silu(x):
    return x * jax.nn.sigmoid(x)


def _dot(a, b):
    return jnp.dot(a, b, preferred_element_type=F32)


def _dot_nt(a, b):
    return lax.dot_general(a, b, (((1,), (1,)), ((), ())), preferred_element_type=F32)


def _dot_tn(a, b):
    return lax.dot_general(a, b, (((0,), (0,)), ((), ())), preferred_element_type=F32)


def _ffn_kernel(*refs, prompt, tm, tiles_per_seq, final_norm, nf):
    if prompt:
        (x_ref, nw_ref, wg_ref, wu_ref, cw_ref, cb_ref, wd_ref, fw_ref,
         o_ref, st_ref, xn_ref, hsa_ref, hsb_ref, p_ref, carry_ref) = refs
    else:
        (x_ref, nw_ref, wg_ref, wu_ref, cw_ref, cb_ref, wd_ref, fw_ref, prev_ref,
         o_ref, hn_ref, xn_ref, hsa_ref, hsb_ref, p_ref) = refs
    m = pl.program_id(0)
    f = pl.program_id(1)
    S = V7X_SUBLANES
    tf = wg_ref.shape[1]
    hs = (hsa_ref, hsb_ref)

    def up_into(dst):
        xn = xn_ref[...]
        for half, w_ref in ((0, wg_ref), (1, wu_ref)):
            h = _dot(xn, w_ref[...])
            dst[half, S:, :] = h
            if not prompt:
                hn_ref[half] = h

    def conv_gate(src, fp):
        if prompt:
            for half in (0, 1):
                seq_start = (m % tiles_per_seq) == 0
                src[half, 0:S, :] = jnp.where(seq_start, 0.0, carry_ref[fp, half])
                carry_ref[fp, half] = src[half, tm:tm + S, :]
                cols = pl.ds(pl.multiple_of(fp * tf, tf), tf)
                for r in range(2):
                    st_ref[0, r, half:half + 1, cols] = src[half, S + tm - 2 + r:S + tm - 1 + r, :]
        rc = min(64, tm)
        cc = min(256, tf)
        for r0 in range(0, tm, rc):
            for c0 in range(0, tf, cc):
                cs = slice(c0, c0 + cc)
                ys = []
                for half in (0, 1):
                    h = src[half, S + r0:S + r0 + rc, cs]
                    if prompt:
                        h1 = src[half, S - 1 + r0:S - 1 + r0 + rc, cs]
                        h2 = src[half, S - 2 + r0:S - 2 + r0 + rc, cs]
                    else:
                        h2 = prev_ref[0, half, r0:r0 + rc, cs]
                        h1 = prev_ref[1, half, r0:r0 + rc, cs]
                    ys.append(cw_ref[2, half:half + 1, cs] * h + cw_ref[1, half:half + 1, cs] * h1
                              + cw_ref[0, half:half + 1, cs] * h2 + cb_ref[half:half + 1, cs])
                p_ref[r0:r0 + rc, cs] = (_silu(ys[0]) * ys[1]).astype(BF16)

    def down():
        o_ref[...] += _dot(p_ref[...], wd_ref[...])

    @pl.when(f == 0)
    def _():
        x = x_ref[...]
        xn_ref[...] = _rms_bf16(x, nw_ref[...])
        o_ref[...] = x
        up_into(hs[0])

    for parity in (0, 1):
        @pl.when(jnp.logical_and(jnp.logical_and(f > 0, f < nf), f % 2 == parity))
        def _():
            conv_gate(hs[1 - parity], f - 1)
            up_into(hs[parity])

            @pl.when(f > 0)
            def _():
                down()

    @pl.when(f == nf)
    def _():
        conv_gate(hs[(nf - 1) % 2], nf - 1)
        down()
        if final_norm:
            o = o_ref[...]
            ms = jnp.mean(o * o, axis=-1, keepdims=True)
            o_ref[...] = o * lax.rsqrt(ms + EPS) * fw_ref[...]


def _ffn(x2d, seq_len, layer, nw, w_up, w_dw, b_dw, w_down, fw, prev=None, final_norm=False):
    T, D = x2d.shape
    F = w_down.shape[1]
    prompt = prev is None
    tm = min(FFN_TM, seq_len) if prompt else T
    tf = FFN_TF if F % FFN_TF == 0 else F
    nf = F // tf
    nm = T // tm
    cw = w_dw.reshape(3, 2, F)
    cb = b_dw.reshape(2, F)
    up_blk = lambda f: jnp.minimum(f, nf - 1)
    dn_blk = lambda f: jnp.maximum(f - 1, 0)
    in_specs = [
        pl.BlockSpec((tm, D), lambda m, f: (m, 0)),
        pl.BlockSpec((1, D), lambda m, f: (0, 0)),
        pl.BlockSpec((None, D, tf), lambda m, f: (layer, 0, up_blk(f))),
        pl.BlockSpec((None, D, tf), lambda m, f: (layer, 0, nf + up_blk(f))),
        pl.BlockSpec((3, 2, tf), lambda m, f: (0, 0, dn_blk(f))),
        pl.BlockSpec((2, tf), lambda m, f: (0, dn_blk(f))),
        pl.BlockSpec((None, tf, D), lambda m, f: (layer, dn_blk(f), 0)),
        pl.BlockSpec((1, D), lambda m, f: (0, 0)),
    ]
    args = [x2d, nw.reshape(1, D), w_up, w_up, cw, cb, w_down, fw.reshape(1, D)]
    hs_buf = pltpu.VMEM((2, tm + V7X_SUBLANES, tf), F32)
    scratch = [pltpu.VMEM((tm, D), BF16), hs_buf, hs_buf, pltpu.VMEM((tm, tf), BF16)]
    if prompt:
        tiles_per_seq = seq_len // tm
        nb = T // seq_len
        out_shape = (jax.ShapeDtypeStruct((T, D), F32),
                     jax.ShapeDtypeStruct((nb, 2, 2, F), F32))
        out_specs = (pl.BlockSpec((tm, D), lambda m, f: (m, 0)),
                     pl.BlockSpec((1, 2, 2, F), lambda m, f: (m // tiles_per_seq, 0, 0, 0)))
        scratch += [pltpu.VMEM((nf, 2, V7X_SUBLANES, tf), F32)]
    else:
        tiles_per_seq = 1
        in_specs.append(pl.BlockSpec((2, 2, tm, tf), lambda m, f: (0, 0, m, dn_blk(f))))
        args.append(prev)
        out_shape = (jax.ShapeDtypeStruct((T, D), F32),
                     jax.ShapeDtypeStruct((2, T, F), F32))
        out_specs = (pl.BlockSpec((tm, D), lambda m, f: (m, 0)),
                     pl.BlockSpec((2, tm, tf), lambda m, f: (0, m, up_blk(f))))
    kern = functools.partial(_ffn_kernel, prompt=prompt, tm=tm, tiles_per_seq=tiles_per_seq,
                             final_norm=final_norm, nf=nf)
    return pl.pallas_call(
        kern, out_shape=out_shape, grid=(nm, nf + 1), in_specs=in_specs, out_specs=out_specs,
        scratch_shapes=scratch, compiler_params=_params(2),
        name="ffn_prompt" if prompt else "ffn_sample")(*args)


def _ln_silu_bf16(c, g, b):
    mu = jnp.mean(c, axis=-1, keepdims=True)
    d = c - mu
    var = jnp.mean(d * d, axis=-1, keepdims=True)
    return _silu(d * lax.rsqrt(var + EPS) * g + b).astype(BF16)


def _conv_prompt_kernel(x_ref, nw_ref, w1_ref, b1_ref, cw_ref, cb_ref, lg_ref, lb_ref,
                        w2_ref, b2_ref, o_ref, st_ref, us_ref, c_ref, cwb_ref,
                        *, tl, nt, dc, width):
    t = pl.program_id(1)
    S = V7X_SUBLANES
    halo = 32
    lead = halo - (width - 1)
    x = x_ref[...]
    xn = _rms_bf16(x, nw_ref[...])
    h = _dot(xn, w1_ref[...]) + b1_ref[...]
    u = h[:, :dc] * jax.nn.sigmoid(h[:, dc:])

    @pl.when(t == 0)
    def _():
        us_ref[0:halo, :] = jnp.zeros((halo, dc), F32)
        for j in range(width):
            cwb_ref[j] = jnp.broadcast_to(cw_ref[j:j + 1, :], (S, dc))

    us_ref[halo:, :] = u

    rb = min(128, tl)
    cwid = V7X_LANES

    def col_body(ci, carry):
        c0 = pl.multiple_of(ci * cwid, cwid)
        cols = pl.ds(c0, cwid)
        for r0 in range(0, tl, rb):
            acc = jnp.broadcast_to(cb_ref[:, cols], (rb // S, S, cwid))
            for a in range(S):
                offs = [lead + j for j in range(width) if (lead + j) % S == a]
                if not offs:
                    continue
                start = offs[0] - a
                span = -(-(rb + offs[-1] - start) // S) * S
                blk = us_ref[pl.ds(r0 + start, span), cols]
                if a:
                    blk = pltpu.roll(blk, span - a, axis=0)
                for off in offs:
                    d = off - offs[0]
                    acc = acc + cwb_ref[off - lead, :, cols] * blk[d:d + rb].reshape(rb // S, S, cwid)
            c_ref[pl.ds(r0, rb), cols] = acc.reshape(rb, cwid)
        return carry

    lax.fori_loop(0, dc // cwid, col_body, 0)

    @pl.when(t == nt - 1)
    def _():
        st_ref[0] = us_ref[halo + tl - (width - 1):halo + tl, :]

    us_ref[0:halo, :] = u[tl - halo:, :]

    c = _ln_silu_bf16(c_ref[...], lg_ref[...], lb_ref[...])
    o_ref[...] = x + _dot(c, w2_ref[...]) + b2_ref[...]


def _conv_prompt(x2d, nb, seq_len, layer, nw, w1, b1, cw, cb, lg, lb, w2, b2):
    T, D = x2d.shape
    dc = w2.shape[1]
    width = cw.shape[0]
    tl = min(CONV_TL, seq_len)
    nt = seq_len // tl
    row = lambda v: v.reshape(1, -1)
    kern = functools.partial(_conv_prompt_kernel, tl=tl, nt=nt, dc=dc, width=width)
    const2 = lambda shape: pl.BlockSpec(shape, lambda b, t: (0, 0))
    return pl.pallas_call(
        kern,
        out_shape=(jax.ShapeDtypeStruct((T, D), F32),
                   jax.ShapeDtypeStruct((nb, width - 1, dc), F32)),
        grid=(nb, nt),
        in_specs=[
            pl.BlockSpec((tl, D), lambda b, t: (b * nt + t, 0)),
            const2((1, D)),
            _resident_layer((D, 2 * dc), layer),
            const2((1, 2 * dc)),
            const2((width, dc)),
            const2((1, dc)), const2((1, dc)), const2((1, dc)),
            _resident_layer((dc, D), layer),
            const2((1, D)),
        ],
        out_specs=(pl.BlockSpec((tl, D), lambda b, t: (b * nt + t, 0)),
                   pl.BlockSpec((1, width - 1, dc), lambda b, t: (b, 0, 0))),
        scratch_shapes=[pltpu.VMEM((32 + tl, dc), F32), pltpu.VMEM((tl, dc), F32),
                        pltpu.VMEM((width, V7X_SUBLANES, dc), F32)],
        compiler_params=_params(2), name="conv_prompt",
    )(x2d, row(nw), w1, row(b1), cw, row(cb), row(lg), row(lb), w2, row(b2))


def _conv_sample_a_kernel(x_ref, nw_ref, wa_ref, wg_ref, ba_ref, bg_ref, cw_ref, cb_ref,
                          st_ref, c_ref, ns_ref, xn_ref, *, width):
    n = pl.program_id(0)

    @pl.when(n == 0)
    def _():
        xn_ref[...] = _rms_bf16(x_ref[...], nw_ref[...])

    xn = xn_ref[...]
    u = (_dot(xn, wa_ref[...]) + ba_ref[...]) * jax.nn.sigmoid(_dot(xn, wg_ref[...]) + bg_ref[...])
    acc = cb_ref[...] + cw_ref[width - 1:width, :] * u
    for j in range(width - 1):
        row = st_ref[:, j, :]
        acc = acc + cw_ref[j:j + 1, :] * row
        if j > 0:
            ns_ref[:, j - 1, :] = row
    ns_ref[:, width - 2, :] = u
    c_ref[...] = acc


def _conv_sample_b_kernel(c_ref, x_ref, lg_ref, lb_ref, w2_ref, b2_ref, o_ref, cn_ref):
    n = pl.program_id(0)

    @pl.when(n == 0)
    def _():
        cn_ref[...] = _ln_silu_bf16(c_ref[...], lg_ref[...], lb_ref[...])

    o_ref[...] = x_ref[...] + _dot(cn_ref[...], w2_ref[...]) + b2_ref[...]


def _conv_sample(x2d, state, layer, nw, w1, b1, cw, cb, lg, lb, w2, b2):
    NB, D = x2d.shape
    dc = w2.shape[1]
    width = cw.shape[0]
    tn = min(256, dc)
    nn = dc // tn
    row = lambda v: v.reshape(1, -1)
    c, new_state = pl.pallas_call(
        functools.partial(_conv_sample_a_kernel, width=width),
        out_shape=(jax.ShapeDtypeStruct((NB, dc), F32),
                   jax.ShapeDtypeStruct((NB, width - 1, dc), F32)),
        grid=(nn,),
        in_specs=[
            pl.BlockSpec((NB, D), lambda n: (0, 0)),
            pl.BlockSpec((1, D), lambda n: (0, 0)),
            pl.BlockSpec((None, D, tn), lambda n: (layer, 0, n)),
            pl.BlockSpec((None, D, tn), lambda n: (layer, 0, nn + n)),
            pl.BlockSpec((1, tn), lambda n: (0, n)),
            pl.BlockSpec((1, tn), lambda n: (0, nn + n)),
            pl.BlockSpec((width, tn), lambda n: (0, n)),
            pl.BlockSpec((1, tn), lambda n: (0, n)),
            pl.BlockSpec((None, NB, width - 1, tn), lambda n: (layer, 0, 0, n)),
        ],
        out_specs=(pl.BlockSpec((NB, tn), lambda n: (0, n)),
                   pl.BlockSpec((NB, width - 1, tn), lambda n: (0, 0, n))),
        scratch_shapes=[pltpu.VMEM((NB, D), BF16)],
        compiler_params=_params(1), name="conv_sample_a",
    )(x2d, row(nw), w1, w1, row(b1), row(b1), cw, row(cb), state)
    tn2 = min(512, D)
    y = pl.pallas_call(
        _conv_sample_b_kernel,
        out_shape=jax.ShapeDtypeStruct((NB, D), F32),
        grid=(D // tn2,),
        in_specs=[
            pl.BlockSpec((NB, dc), lambda n: (0, 0)),
            pl.BlockSpec((NB, tn2), lambda n: (0, n)),
            pl.BlockSpec((1, dc), lambda n: (0, 0)),
            pl.BlockSpec((1, dc), lambda n: (0, 0)),
            pl.BlockSpec((None, dc, tn2), lambda n: (layer, 0, n)),
            pl.BlockSpec((1, tn2), lambda n: (0, n)),
        ],
        out_specs=pl.BlockSpec((NB, tn2), lambda n: (0, n)),
        scratch_shapes=[pltpu.VMEM((NB, dc), BF16)],
        compiler_params=_params(1), name="conv_sample_b",
    )(c, x2d, row(lg), row(lb), w2, row(b2))
    return y, new_state


def _proj_kernel(x_ref, nw_ref, w_ref, o_ref, xn_ref, *, acts):
    s = pl.program_id(1)

    @pl.when(s == 0)
    def _():
        xn_ref[...] = _rms_bf16(x_ref[...], nw_ref[...])

    silu_segs = [i for i, a in enumerate(acts) if a]
    is_silu = functools.reduce(jnp.logical_or, [s == i for i in silu_segs])
    n = w_ref.shape[2]
    cw = 512 if n % 512 == 0 else n

    def segment(act):
        xn = xn_ref[...]
        for c0 in range(0, n, cw):
            y = _dot(xn, w_ref[0, :, c0:c0 + cw])
            o_ref[0, :, c0:c0 + cw] = _silu(y) if act else y

    @pl.when(is_silu)
    def _():
        segment(True)

    @pl.when(jnp.logical_not(is_silu))
    def _():
        segment(False)


def _proj(x2d, nw, w_stack, acts, tm):
    T, D = x2d.shape
    S, _, N = w_stack.shape
    return pl.pallas_call(
        functools.partial(_proj_kernel, acts=acts),
        out_shape=jax.ShapeDtypeStruct((S, T, N), F32),
        grid=(T // tm, S),
        in_specs=[pl.BlockSpec((tm, D), lambda m, s: (m, 0)),
                  pl.BlockSpec((1, D), lambda m, s: (0, 0)),
                  pl.BlockSpec((1, D, N), lambda m, s: (s, 0, 0))],
        out_specs=pl.BlockSpec((1, tm, N), lambda m, s: (s, m, 0)),
        scratch_shapes=[pltpu.VMEM((tm, D), BF16)],
        compiler_params=_params(2), name="hgrn_proj",
    )(x2d, nw.reshape(1, D), w_stack)


def _residual_matmul_kernel(a_ref, w_ref, x_ref, o_ref):
    o_ref[...] = x_ref[...] + _dot(a_ref[...], w_ref[...])


def _residual_matmul(a, w, layer, x2d, tm):
    T, K = a.shape
    N = w.shape[2]
    return pl.pallas_call(
        _residual_matmul_kernel,
        out_shape=jax.ShapeDtypeStruct((T, N), F32),
        grid=(T // tm,),
        in_specs=[pl.BlockSpec((tm, K), lambda m: (m, 0)),
                  _resident_layer((K, N), layer),
                  pl.BlockSpec((tm, N), lambda m: (m, 0))],
        out_specs=pl.BlockSpec((tm, N), lambda m: (m, 0)),
        compiler_params=_params(1), name="hgrn_out",
    )(a, w, x2d)


def _forget_lower_bound(raw, layer):
    mx = jnp.max(raw, axis=0, keepdims=True)
    e = jnp.exp(raw - mx)
    sm = e / jnp.sum(e, axis=0, keepdims=True)
    first = sm[0:1]
    cum = first
    for i in range(1, layer + 1):
        cum = cum + sm[i:i + 1]
    return cum - first


def _log_forget_and_key(z, lb):
    t = jnp.exp(-jnp.abs(z))
    r = 1.0 / (1.0 + t)
    sig_neg = jnp.where(z >= 0.0, t * r, r)
    log_sig = jnp.minimum(z, 0.0) - jnp.log(1.0 + t)
    a = jnp.log(lb)
    b = jnp.log1p(-lb) + log_sig
    log_f = jnp.maximum(a, b) + jnp.log(1.0 + jnp.exp(-jnp.abs(a - b)))
    k = (1.0 - lb) * sig_neg
    return log_f, k


def _split3_bf16(x):
    hi = x.astype(BF16)
    r1 = x - hi.astype(F32)
    mid = r1.astype(BF16)
    lo = (r1 - mid.astype(F32)).astype(BF16)
    return hi, mid, lo


def _head_norm_gate(o, ng, gate):
    ms = jnp.mean(o * o, axis=-1, keepdims=True)
    return (o * lax.rsqrt(ms + EPS) * ng * gate).astype(BF16)


def _gla_prompt_kernel(q_ref, z_ref, v_ref, g_ref, lbr_ref, ng_ref, o_ref, s_ref,
                       st_ref, gh_ref, oacc_ref, x_ref, w_ref, ad_ref, *, layer, tl, nt, hg):
    C = min(GLA_CHUNK, tl)
    sub = min(GLA_SUB, C)
    n_sub = C // sub
    W = V7X_LANES
    t_idx = pl.program_id(2)

    @pl.when(t_idx == 0)
    def _():
        st_ref[...] = jnp.zeros(st_ref.shape, F32)
        r = lax.broadcasted_iota(jnp.int32, (sub * W, W), 0) // W
        c = lax.broadcasted_iota(jnp.int32, (sub * W, W), 1)
        for i in range(n_sub):
            w_ref[i] = (c == r + i * sub).astype(BF16)

    lb_all = _forget_lower_bound(lbr_ref[...], layer)
    tri = (lax.broadcasted_iota(jnp.int32, (C, C), 0)
           >= lax.broadcasted_iota(jnp.int32, (C, C), 1)).astype(BF16)
    a_row = lax.broadcasted_iota(jnp.int32, (sub, C), 0)
    a_col = lax.broadcasted_iota(jnp.int32, (sub, C), 1)
    zero_rows = jnp.zeros((V7X_SUBLANES, W), F32)

    n_chunks = tl // C

    def chunk_rows(ci):
        start = ci * C if isinstance(ci, int) else pl.multiple_of(ci * C, C)
        return pl.ds(start, C)

    def decay_terms(ci, slot):
        rows = chunk_rows(ci)
        log_f, k = _log_forget_and_key(z_ref[0, 0, rows, :], lb_all)
        hi, mid, lo = _split3_bf16(log_f)
        G = (_dot(tri, hi) + _dot(tri, mid) + _dot(tri, lo)) * LOG2E
        gh_ref[slot, 0] = G
        gh_ref[slot, 1] = G - jnp.log2(k)
        gh_ref[slot, 2] = k

    def dense(ci, slot):
        rows = chunk_rows(ci)
        for hh in range(hg):
            ls = slice(hh * W, (hh + 1) * W)
            q = q_ref[0, 0, rows, ls]
            v = v_ref[0, 0, rows, ls].astype(BF16)
            G = gh_ref[slot, 0, :, ls]
            k = gh_ref[slot, 2, :, ls]
            g_last = G[C - 1:C, :]
            st = st_ref[hh]
            o = _dot_nt((q * jnp.exp2(G)).astype(BF16), st.astype(BF16))
            off_blocks = [jnp.zeros((sub, C), BF16)]
            for i in range(n_sub):
                base = i * sub
                Gi, qi = G[base:base + sub], q[base:base + sub]
                for s in range(sub):
                    lo_row = (s // V7X_SUBLANES) * V7X_SUBLANES
                    h_s = gh_ref[slot, 1, base + s:base + s + 1, ls]
                    e = jnp.exp2(jnp.minimum(Gi[lo_row:] - h_s, 0.0)) * qi[lo_row:]
                    if lo_row:
                        e = jnp.concatenate([zero_rows] * (lo_row // V7X_SUBLANES) + [e], axis=0)
                    x_ref[i, hh * sub:(hh + 1) * sub, s * W:(s + 1) * W] = e.astype(BF16)
                if i > 0:
                    g_prev = gh_ref[slot, 0, base - 1:base, ls]
                    qt = (qi * jnp.exp2(Gi - g_prev)).astype(BF16)
                    kt = jnp.concatenate(
                        [(k[:base] * jnp.exp2(g_prev - G[:base])).astype(BF16),
                         jnp.zeros((C - base, W), BF16)], axis=0)
                    off_blocks.append(_dot_nt(qt, kt).astype(BF16))
            a_off = jnp.concatenate(off_blocks, axis=0) if n_sub > 1 else off_blocks[0]
            oacc_ref[hh] = o + _dot(a_off, v)
            kd = (k * jnp.exp2(g_last - G)).astype(BF16)
            st_ref[hh] = jnp.exp2(g_last) * st + _dot_tn(v, kd)

    def finish(ci):
        rows = chunk_rows(ci)
        for i in range(n_sub):
            ad_ref[i] = _dot(x_ref[i], w_ref[i])
        for hh in range(hg):
            ls = slice(hh * W, (hh + 1) * W)
            blocks = [jnp.where(a_col <= a_row + i * sub,
                                ad_ref[i, hh * sub:(hh + 1) * sub, 0:C], 0.0).astype(BF16)
                      for i in range(n_sub)]
            a_diag = jnp.concatenate(blocks, axis=0) if n_sub > 1 else blocks[0]
            o = oacc_ref[hh] + _dot(a_diag, v_ref[0, 0, rows, ls].astype(BF16))
            o_ref[0, rows, ls] = _head_norm_gate(o, ng_ref[:, ls], g_ref[0, 0, rows, ls])

    decay_terms(0, 0)

    def chunk(ci, carry):
        slot = ci % 2
        dense(ci, slot)
        finish(ci)
        decay_terms(jnp.minimum(ci + 1, n_chunks - 1), 1 - slot)
        return carry

    lax.fori_loop(0, n_chunks, chunk, 0)

    @pl.when(t_idx == nt - 1)
    def _():
        for hh in range(hg):
            s_ref[0, hh] = st_ref[hh].T


def _gla_prompt(P, lb_raw, layer, ng, n_heads):
    _, B, L, D = P.shape
    W = D // n_heads
    assert W == V7X_LANES
    hg = min(GLA_HEADS_PER_STEP, n_heads)
    tl = min(GLA_TL, L)
    nt = L // tl
    C = min(GLA_CHUNK, tl)
    sub = min(GLA_SUB, C)
    n_sub = C // sub
    NL = lb_raw.shape[0]
    seg = lambda s: pl.BlockSpec((1, 1, tl, hg * W), lambda b, h, t: (s, b, t, h))
    return pl.pallas_call(
        functools.partial(_gla_prompt_kernel, layer=layer, tl=tl, nt=nt, hg=hg),
        out_shape=(jax.ShapeDtypeStruct((B, L, D), BF16),
                   jax.ShapeDtypeStruct((B, n_heads, W, W), F32)),
        grid=(B, n_heads // hg, nt),
        in_specs=[seg(0), seg(1), seg(2), seg(3),
                  pl.BlockSpec((NL, hg * W), lambda b, h, t: (0, h)),
                  pl.BlockSpec((1, hg * W), lambda b, h, t: (0, h))],
        out_specs=(pl.BlockSpec((1, tl, hg * W), lambda b, h, t: (b, t, h)),
                   pl.BlockSpec((1, hg, W, W), lambda b, h, t: (b, h, 0, 0))),
        scratch_shapes=[pltpu.VMEM((hg, W, W), F32),
                        pltpu.VMEM((2, 3, C, hg * W), F32),
                        pltpu.VMEM((hg, C, W), F32),
                        pltpu.VMEM((n_sub, hg * sub, sub * W), BF16),
                        pltpu.VMEM((n_sub, sub * W, W), BF16),
                        pltpu.VMEM((n_sub, hg * sub, W), F32)],
        compiler_params=_params(3), name="gla_prompt",
    )(P, P, P, P, lb_raw, ng.reshape(1, D))


def _gla_sample_kernel(q_ref, z_ref, v_ref, g_ref, lbr_ref, ng_ref, s_ref, o_ref, ns_ref,
                       qt_ref, ft_ref, kt_ref, oacc_ref, *, layer, bb):
    j = pl.program_id(1)

    @pl.when(j == 0)
    def _():
        lb = _forget_lower_bound(lbr_ref[...], layer)
        log_f, k = _log_forget_and_key(z_ref[0], lb)
        qt_ref[...] = q_ref[0].T
        ft_ref[...] = jnp.exp(log_f).T
        kt_ref[...] = k.T

    shift = (V7X_LANES - j * bb) % V7X_LANES
    qt = pltpu.roll(qt_ref[...], shift, axis=1)
    ft = pltpu.roll(ft_ref[...], shift, axis=1)
    kt = pltpu.roll(kt_ref[...], shift, axis=1)
    r0 = pl.multiple_of(j * bb, bb)
    v_blk = v_ref[0, pl.ds(r0, bb), :]
    for i in range(bb):
        s_new = ft[:, i:i + 1] * s_ref[i, 0] + kt[:, i:i + 1] * v_blk[i:i + 1, :]
        ns_ref[i, 0] = s_new
        oacc_ref[i:i + 1, :] = jnp.sum(qt[:, i:i + 1] * s_new, axis=0, keepdims=True)
    o_ref[...] = _head_norm_gate(oacc_ref[...], ng_ref[...], g_ref[0, pl.ds(r0, bb), :])


def _gla_sample(P, state, lb_raw, layer, ng):
    _, NB, D = P.shape
    _, _, H, DK, DV = state.shape
    assert NB == V7X_LANES and DK == V7X_LANES and DV == V7X_LANES
    bb = 32
    NL = lb_raw.shape[0]
    seg = lambda s: pl.BlockSpec((1, NB, DK), lambda h, j: (s, 0, h))
    return pl.pallas_call(
        functools.partial(_gla_sample_kernel, layer=layer, bb=bb),
        out_shape=(jax.ShapeDtypeStruct((NB, D), BF16),
                   jax.ShapeDtypeStruct(state.shape[1:], F32)),
        grid=(H, NB // bb),
        in_specs=[seg(0), seg(1), seg(2), seg(3),
                  pl.BlockSpec((NL, DK), lambda h, j: (0, h)),
                  pl.BlockSpec((1, DV), lambda h, j: (0, h)),
                  pl.BlockSpec((None, bb, 1, DK, DV), lambda h, j: (layer, j, h, 0, 0))],
        out_specs=(pl.BlockSpec((bb, DV), lambda h, j: (j, h)),
                   pl.BlockSpec((bb, 1, DK, DV), lambda h, j: (j, h, 0, 0))),
        scratch_shapes=[pltpu.VMEM((DK, NB), F32)] * 3 + [pltpu.VMEM((bb, DV), F32)],
        compiler_params=_params(2), name="gla_sample",
    )(P, P, P, P, lb_raw, ng.reshape(1, D), state)


def kernel(x_prompt, x_sample, state_conv, state_hgrn, state_ffn, norm_mix, norm_ffn, norm_final,
           conv_w_pw1, conv_b_pw1, conv_w_dw, conv_b_dw, conv_ln_g, conv_ln_b, conv_w_pw2,
           conv_b_pw2, hgrn_w_q, hgrn_w_f, hgrn_w_i, hgrn_w_g, hgrn_w_o, hgrn_lb_raw,
           hgrn_norm_g, ffn_w_up, ffn_w_dw, ffn_b_dw, ffn_w_down):
    B, L, D = x_prompt.shape
    NB = x_sample.shape[0]
    depth = ffn_w_up.shape[0]
    n_heads = state_hgrn.shape[2]
    F = ffn_w_down.shape[1]
    lb_raw = hgrn_lb_raw.astype(F32)

    xp = x_prompt.reshape(B * L, D)
    xs = x_sample.reshape(NB, D)
    conv_p, hgrn_p, ffn_p, conv_s, hgrn_s, ffn_s = [], [], [], [], [], []
    acts = (True, False, False, True)
    w_pw1 = conv_w_pw1.astype(BF16)
    w_pw2 = conv_w_pw2.astype(BF16)
    w_o = hgrn_w_o.astype(BF16)
    w_up = ffn_w_up.astype(BF16)
    w_down = ffn_w_down.astype(BF16)
    prev_all = state_ffn.reshape(depth, NB, 2, 2, F).transpose(0, 2, 3, 1, 4)
    for i in range(depth):
        j = i // 2
        if i % 2 == 0:
            cargs = (j, norm_mix[i], w_pw1, conv_b_pw1[j], conv_w_dw[j], conv_b_dw[j],
                     conv_ln_g[j], conv_ln_b[j], w_pw2, conv_b_pw2[j])
            xp, st = _conv_prompt(xp, B, L, *cargs)
            conv_p.append(st)
            xs, st = _conv_sample(xs, state_conv, *cargs)
            conv_s.append(st)
        else:
            w_stack = jnp.stack([hgrn_w_q[j], hgrn_w_f[j], hgrn_w_i[j], hgrn_w_g[j]]).astype(BF16)
            P = _proj(xp, norm_mix[i], w_stack, acts, tm=min(PROJ_TM, L))
            og, st = _gla_prompt(P.reshape(4, B, L, D), lb_raw, j, hgrn_norm_g[j], n_heads)
            hgrn_p.append(st)
            xp = _residual_matmul(og.reshape(B * L, D), w_o, j, xp, tm=min(PROJ_TM, L))
            P = _proj(xs, norm_mix[i], w_stack, acts, tm=NB)
            og, st = _gla_sample(P, state_hgrn, lb_raw, j, hgrn_norm_g[j])
            hgrn_s.append(st)
            xs = _residual_matmul(og, w_o, j, xs, tm=NB)
        last = i == depth - 1
        fargs = (i, norm_ffn[i], w_up, ffn_w_dw[i], ffn_b_dw[i], w_down, norm_final)
        xp, st = _ffn(xp, L, *fargs, final_norm=last)
        ffn_p.append(st.reshape(B, 2, 2 * F))
        prev = prev_all[i]
        xs, hn = _ffn(xs, 1, *fargs, prev=prev, final_norm=last)
        new_prev = jnp.stack([prev[1], hn], axis=0)
        ffn_s.append(new_prev.transpose(2, 0, 1, 3).reshape(NB, 2, 2 * F))
    return (xp.reshape(B, L, D), xs.reshape(NB, 1, D),
            jnp.stack(conv_p), jnp.stack(hgrn_p), jnp.stack(ffn_p),
            jnp.stack(conv_s), jnp.stack(hgrn_s), jnp.stack(ffn_s))
```

```python
import functools

import jax
import jax.numpy as jnp
from jax import lax
from jax.experimental import pallas as pl
from jax.experimental.pallas import tpu as pltpu

F32 = jnp.float32
BF16 = jnp.bfloat16
EPS = 1e-6
LOG2E = 1.4426950408889634

V7X_VMEM_BYTES = 64 * 1024 * 1024
V7X_LANES = 128
V7X_SUBLANES = 8
V7X_MXU_DIM = 256
VMEM_LIMIT = V7X_VMEM_BYTES - 8 * 1024 * 1024

GLA_CHUNK = 64
GLA_SUB = 16
GLA_HEADS_PER_STEP = 8
GLA_TL = 1024
FFN_TM = 512
FFN_TF = 512
CONV_TL = 256
PROJ_TM = 512


def _params(n_axes, vmem=VMEM_LIMIT):
    return pltpu.CompilerParams(
        dimension_semantics=("arbitrary",) * n_axes, vmem_limit_bytes=vmem)


def _resident_layer(shape, layer):
    n = len(shape)
    return pl.BlockSpec((None,) + tuple(shape), lambda *_: (layer,) + (0,) * n,
                        pipeline_mode=pl.Buffered(1))


def _rms_bf16(x, w):
    ms = jnp.mean(x * x, axis=-1, keepdims=True)
    return (x * lax.rsqrt(ms + EPS) * w).astype(BF16)


def _silu(x):
    return x * jax.nn.sigmoid(x)


def _dot(a, b):
    return jnp.dot(a, b, preferred_element_type=F32)


def _dot_nt(a, b):
    return lax.dot_general(a, b, (((1,), (1,)), ((), ())), preferred_element_type=F32)


def _dot_tn(a, b):
    return lax.dot_general(a, b, (((0,), (0,)), ((), ())), preferred_element_type=F32)


def _ffn_kernel(*refs, prompt, tm, tiles_per_seq, final_norm, nf):
    if prompt:
        (x_ref, nw_ref, wg_ref, wu_ref, cw_ref, cb_ref, wd_ref, fw_ref,
         o_ref, st_ref, xn_ref, hs_ref, p_ref, carry_ref) = refs
    else:
        (x_ref, nw_ref, wg_ref, wu_ref, cw_ref, cb_ref, wd_ref, fw_ref, prev_ref,
         o_ref, hn_ref, xn_ref, hs_ref, p_ref) = refs
    m = pl.program_id(0)
    f = pl.program_id(1)
    S = V7X_SUBLANES
    tf = wg_ref.shape[1]

    def up_into(dst):
        xn = xn_ref[...]
        for half, w_ref in ((0, wg_ref), (1, wu_ref)):
            h = _dot(xn, w_ref[...])
            dst[half, S:, :] = h
            if not prompt:
                hn_ref[half] = h

    def conv_gate(src, fp):
        if prompt:
            for half in (0, 1):
                seq_start = (m % tiles_per_seq) == 0
                src[half, 0:S, :] = jnp.where(seq_start, 0.0, carry_ref[fp, half])
                carry_ref[fp, half] = src[half, tm:tm + S, :]
                cols = pl.ds(pl.multiple_of(fp * tf, tf), tf)
                for r in range(2):
                    st_ref[0, r, half:half + 1, cols] = src[half, S + tm - 2 + r:S + tm - 1 + r, :]
        rc = min(64, tm)
        cc = min(256, tf)
        for r0 in range(0, tm, rc):
            for c0 in range(0, tf, cc):
                cs = slice(c0, c0 + cc)
                ys = []
                for half in (0, 1):
                    h = src[half, S + r0:S + r0 + rc, cs]
                    if prompt:
                        h1 = src[half, S - 1 + r0:S - 1 + r0 + rc, cs]
                        h2 = src[half, S - 2 + r0:S - 2 + r0 + rc, cs]
                    else:
                        h2 = prev_ref[0, half, r0:r0 + rc, cs]
                        h1 = prev_ref[1, half, r0:r0 + rc, cs]
                    ys.append(cw_ref[2, half:half + 1, cs] * h + cw_ref[1, half:half + 1, cs] * h1
                              + cw_ref[0, half:half + 1, cs] * h2 + cb_ref[half:half + 1, cs])
                p_ref[r0:r0 + rc, cs] = (_silu(ys[0]) * ys[1]).astype(BF16)

    @pl.when(f == 0)
    def _():
        x = x_ref[...]
        xn_ref[...] = _rms_bf16(x, nw_ref[...])
        o_ref[...] = x

    up_into(hs_ref)
    conv_gate(hs_ref, f)
    o_ref[...] += _dot(p_ref[...], wd_ref[...])

    if final_norm:
        @pl.when(f == nf - 1)
        def _():
            o = o_ref[...]
            ms = jnp.mean(o * o, axis=-1, keepdims=True)
            o_ref[...] = o * lax.rsqrt(ms + EPS) * fw_ref[...]


def _ffn(x2d, seq_len, layer, nw, w_up, w_dw, b_dw, w_down, fw, prev=None, final_norm=False):
    T, D = x2d.shape
    F = w_down.shape[1]
    prompt = prev is None
    tm = min(FFN_TM, seq_len) if prompt else T
    tf = FFN_TF if F % FFN_TF == 0 else F
    nf = F // tf
    nm = T // tm
    cw = w_dw.reshape(3, 2, F)
    cb = b_dw.reshape(2, F)
    in_specs = [
        pl.BlockSpec((tm, D), lambda m, f: (m, 0)),
        pl.BlockSpec((1, D), lambda m, f: (0, 0)),
        pl.BlockSpec((None, D, tf), lambda m, f: (layer, 0, f)),
        pl.BlockSpec((None, D, tf), lambda m, f: (layer, 0, nf + f)),
        pl.BlockSpec((3, 2, tf), lambda m, f: (0, 0, f)),
        pl.BlockSpec((2, tf), lambda m, f: (0, f)),
        pl.BlockSpec((None, tf, D), lambda m, f: (layer, f, 0)),
        pl.BlockSpec((1, D), lambda m, f: (0, 0)),
    ]
    args = [x2d, nw.reshape(1, D), w_up, w_up, cw, cb, w_down, fw.reshape(1, D)]
    scratch = [pltpu.VMEM((tm, D), BF16), pltpu.VMEM((2, tm + V7X_SUBLANES, tf), F32),
               pltpu.VMEM((tm, tf), BF16)]
    if prompt:
        tiles_per_seq = seq_len // tm
        nb = T // seq_len
        out_shape = (jax.ShapeDtypeStruct((T, D), F32),
                     jax.ShapeDtypeStruct((nb, 2, 2, F), F32))
        out_specs = (pl.BlockSpec((tm, D), lambda m, f: (m, 0)),
                     pl.BlockSpec((1, 2, 2, F), lambda m, f: (m // tiles_per_seq, 0, 0, 0)))
        scratch += [pltpu.VMEM((nf, 2, V7X_SUBLANES, tf), F32)]
    else:
        tiles_per_seq = 1
        in_specs.append(pl.BlockSpec((2, 2, tm, tf), lambda m, f: (0, 0, m, f)))
        args.append(prev)
        out_shape = (jax.ShapeDtypeStruct((T, D), F32),
                     jax.ShapeDtypeStruct((2, T, F), F32))
        out_specs = (pl.BlockSpec((tm, D), lambda m, f: (m, 0)),
                     pl.BlockSpec((2, tm, tf), lambda m, f: (0, m, f)))
    kern = functools.partial(_ffn_kernel, prompt=prompt, tm=tm, tiles_per_seq=tiles_per_seq,
                             final_norm=final_norm, nf=nf)
    return pl.pallas_call(
        kern, out_shape=out_shape, grid=(nm, nf), in_specs=in_specs, out_specs=out_specs,
        scratch_shapes=scratch, compiler_params=_params(2),
        name="ffn_prompt" if prompt else "ffn_sample")(*args)


def _ln_silu_bf16(c, g, b):
    mu = jnp.mean(c, axis=-1, keepdims=True)
    d = c - mu
    var = jnp.mean(d * d, axis=-1, keepdims=True)
    return _silu(d * lax.rsqrt(var + EPS) * g + b).astype(BF16)


def _conv_prompt_kernel(x_ref, nw_ref, w1_ref, b1_ref, cw_ref, cb_ref, lg_ref, lb_ref,
                        w2_ref, b2_ref, o_ref, st_ref, us_ref, c_ref, cwb_ref,
                        *, tl, nt, dc, width):
    t = pl.program_id(1)
    S = V7X_SUBLANES
    halo = 32
    lead = halo - (width - 1)
    x = x_ref[...]
    xn = _rms_bf16(x, nw_ref[...])
    h = _dot(xn, w1_ref[...]) + b1_ref[...]
    u = h[:, :dc] * jax.nn.sigmoid(h[:, dc:])

    @pl.when(t == 0)
    def _():
        us_ref[0:halo, :] = jnp.zeros((halo, dc), F32)
        for j in range(width):
            cwb_ref[j] = jnp.broadcast_to(cw_ref[j:j + 1, :], (S, dc))

    us_ref[halo:, :] = u

    rb = min(128, tl)
    cwid = V7X_LANES

    def col_body(ci, carry):
        c0 = pl.multiple_of(ci * cwid, cwid)
        cols = pl.ds(c0, cwid)
        for r0 in range(0, tl, rb):
            acc = jnp.broadcast_to(cb_ref[:, cols], (rb // S, S, cwid))
            for a in range(S):
                offs = [lead + j for j in range(width) if (lead + j) % S == a]
                if not offs:
                    continue
                start = offs[0] - a
                span = -(-(rb + offs[-1] - start) // S) * S
                blk = us_ref[pl.ds(r0 + start, span), cols]
                if a:
                    blk = pltpu.roll(blk, span - a, axis=0)
                for off in offs:
                    d = off - offs[0]
                    acc = acc + cwb_ref[off - lead, :, cols] * blk[d:d + rb].reshape(rb // S, S, cwid)
            c_ref[pl.ds(r0, rb), cols] = acc.reshape(rb, cwid)
        return carry

    lax.fori_loop(0, dc // cwid, col_body, 0)

    @pl.when(t == nt - 1)
    def _():
        st_ref[0] = us_ref[halo + tl - (width - 1):halo + tl, :]

    us_ref[0:halo, :] = u[tl - halo:, :]

    c = _ln_silu_bf16(c_ref[...], lg_ref[...], lb_ref[...])
    o_ref[...] = x + _dot(c, w2_ref[...]) + b2_ref[...]


def _conv_prompt(x2d, nb, seq_len, layer, nw, w1, b1, cw, cb, lg, lb, w2, b2):
    T, D = x2d.shape
    dc = w2.shape[1]
    width = cw.shape[0]
    tl = min(CONV_TL, seq_len)
    nt = seq_len // tl
    row = lambda v: v.reshape(1, -1)
    kern = functools.partial(_conv_prompt_kernel, tl=tl, nt=nt, dc=dc, width=width)
    const2 = lambda shape: pl.BlockSpec(shape, lambda b, t: (0, 0))
    return pl.pallas_call(
        kern,
        out_shape=(jax.ShapeDtypeStruct((T, D), F32),
                   jax.ShapeDtypeStruct((nb, width - 1, dc), F32)),
        grid=(nb, nt),
        in_specs=[
            pl.BlockSpec((tl, D), lambda b, t: (b * nt + t, 0)),
            const2((1, D)),
            _resident_layer((D, 2 * dc), layer),
            const2((1, 2 * dc)),
            const2((width, dc)),
            const2((1, dc)), const2((1, dc)), const2((1, dc)),
            _resident_layer((dc, D), layer),
            const2((1, D)),
        ],
        out_specs=(pl.BlockSpec((tl, D), lambda b, t: (b * nt + t, 0)),
                   pl.BlockSpec((1, width - 1, dc), lambda b, t: (b, 0, 0))),
        scratch_shapes=[pltpu.VMEM((32 + tl, dc), F32), pltpu.VMEM((tl, dc), F32),
                        pltpu.VMEM((width, V7X_SUBLANES, dc), F32)],
        compiler_params=_params(2), name="conv_prompt",
    )(x2d, row(nw), w1, row(b1), cw, row(cb), row(lg), row(lb), w2, row(b2))


def _conv_sample_a_kernel(x_ref, nw_ref, wa_ref, wg_ref, ba_ref, bg_ref, cw_ref, cb_ref,
                          st_ref, c_ref, ns_ref, xn_ref, *, width):
    n = pl.program_id(0)

    @pl.when(n == 0)
    def _():
        xn_ref[...] = _rms_bf16(x_ref[...], nw_ref[...])

    xn = xn_ref[...]
    u = (_dot(xn, wa_ref[...]) + ba_ref[...]) * jax.nn.sigmoid(_dot(xn, wg_ref[...]) + bg_ref[...])
    acc = cb_ref[...] + cw_ref[width - 1:width, :] * u
    for j in range(width - 1):
        row = st_ref[j]
        acc = acc + cw_ref[j:j + 1, :] * row
        if j > 0:
            ns_ref[j - 1] = row
    ns_ref[width - 2] = u
    c_ref[...] = acc


def _conv_sample_b_kernel(c_ref, x_ref, lg_ref, lb_ref, w2_ref, b2_ref, o_ref, cn_ref):
    n = pl.program_id(0)

    @pl.when(n == 0)
    def _():
        cn_ref[...] = _ln_silu_bf16(c_ref[...], lg_ref[...], lb_ref[...])

    o_ref[...] = x_ref[...] + _dot(cn_ref[...], w2_ref[...]) + b2_ref[...]


def _conv_sample(x2d, state, layer, nw, w1, b1, cw, cb, lg, lb, w2, b2):
    NB, D = x2d.shape
    dc = w2.shape[1]
    width = cw.shape[0]
    tn = min(256, dc)
    nn = dc // tn
    row = lambda v: v.reshape(1, -1)
    c, new_state = pl.pallas_call(
        functools.partial(_conv_sample_a_kernel, width=width),
        out_shape=(jax.ShapeDtypeStruct((NB, dc), F32),
                   jax.ShapeDtypeStruct((width - 1, NB, dc), F32)),
        grid=(nn,),
        in_specs=[
            pl.BlockSpec((NB, D), lambda n: (0, 0)),
            pl.BlockSpec((1, D), lambda n: (0, 0)),
            pl.BlockSpec((None, D, tn), lambda n: (layer, 0, n)),
            pl.BlockSpec((None, D, tn), lambda n: (layer, 0, nn + n)),
            pl.BlockSpec((1, tn), lambda n: (0, n)),
            pl.BlockSpec((1, tn), lambda n: (0, nn + n)),
            pl.BlockSpec((width, tn), lambda n: (0, n)),
            pl.BlockSpec((1, tn), lambda n: (0, n)),
            pl.BlockSpec((None, width - 1, NB, tn), lambda n: (layer, 0, 0, n)),
        ],
        out_specs=(pl.BlockSpec((NB, tn), lambda n: (0, n)),
                   pl.BlockSpec((width - 1, NB, tn), lambda n: (0, 0, n))),
        scratch_shapes=[pltpu.VMEM((NB, D), BF16)],
        compiler_params=_params(1), name="conv_sample_a",
    )(x2d, row(nw), w1, w1, row(b1), row(b1), cw, row(cb), state)
    tn2 = min(512, D)
    y = pl.pallas_call(
        _conv_sample_b_kernel,
        out_shape=jax.ShapeDtypeStruct((NB, D), F32),
        grid=(D // tn2,),
        in_specs=[
            pl.BlockSpec((NB, dc), lambda n: (0, 0)),
            pl.BlockSpec((NB, tn2), lambda n: (0, n)),
            pl.BlockSpec((1, dc), lambda n: (0, 0)),
            pl.BlockSpec((1, dc), lambda n: (0, 0)),
            pl.BlockSpec((None, dc, tn2), lambda n: (layer, 0, n)),
            pl.BlockSpec((1, tn2), lambda n: (0, n)),
        ],
        out_specs=pl.BlockSpec((NB, tn2), lambda n: (0, n)),
        scratch_shapes=[pltpu.VMEM((NB, dc), BF16)],
        compiler_params=_params(1), name="conv_sample_b",
    )(c, x2d, row(lg), row(lb), w2, row(b2))
    return y, new_state


def _proj_kernel(x_ref, nw_ref, w_ref, o_ref, xn_ref, *, acts):
    s = pl.program_id(1)

    @pl.when(s == 0)
    def _():
        xn_ref[...] = _rms_bf16(x_ref[...], nw_ref[...])

    silu_segs = [i for i, a in enumerate(acts) if a]
    is_silu = functools.reduce(jnp.logical_or, [s == i for i in silu_segs])
    n = w_ref.shape[2]
    cw = 512 if n % 512 == 0 else n

    def segment(act):
        xn = xn_ref[...]
        for c0 in range(0, n, cw):
            y = _dot(xn, w_ref[0, :, c0:c0 + cw])
            o_ref[0, :, c0:c0 + cw] = _silu(y) if act else y

    @pl.when(is_silu)
    def _():
        segment(True)

    @pl.when(jnp.logical_not(is_silu))
    def _():
        segment(False)


def _proj(x2d, nw, w_stack, acts, tm):
    T, D = x2d.shape
    S, _, N = w_stack.shape
    return pl.pallas_call(
        functools.partial(_proj_kernel, acts=acts),
        out_shape=jax.ShapeDtypeStruct((S, T, N), F32),
        grid=(T // tm, S),
        in_specs=[pl.BlockSpec((tm, D), lambda m, s: (m, 0)),
                  pl.BlockSpec((1, D), lambda m, s: (0, 0)),
                  pl.BlockSpec((1, D, N), lambda m, s: (s, 0, 0))],
        out_specs=pl.BlockSpec((1, tm, N), lambda m, s: (s, m, 0)),
        scratch_shapes=[pltpu.VMEM((tm, D), BF16)],
        compiler_params=_params(2), name="hgrn_proj",
    )(x2d, nw.reshape(1, D), w_stack)


def _residual_matmul_kernel(a_ref, w_ref, x_ref, o_ref):
    o_ref[...] = x_ref[...] + _dot(a_ref[...], w_ref[...])


def _residual_matmul(a, w, layer, x2d, tm):
    T, K = a.shape
    N = w.shape[2]
    return pl.pallas_call(
        _residual_matmul_kernel,
        out_shape=jax.ShapeDtypeStruct((T, N), F32),
        grid=(T // tm,),
        in_specs=[pl.BlockSpec((tm, K), lambda m: (m, 0)),
                  _resident_layer((K, N), layer),
                  pl.BlockSpec((tm, N), lambda m: (m, 0))],
        out_specs=pl.BlockSpec((tm, N), lambda m: (m, 0)),
        compiler_params=_params(1), name="hgrn_out",
    )(a, w, x2d)


def _forget_lower_bound(raw, layer):
    mx = jnp.max(raw, axis=0, keepdims=True)
    e = jnp.exp(raw - mx)
    sm = e / jnp.sum(e, axis=0, keepdims=True)
    first = sm[0:1]
    cum = first
    for i in range(1, layer + 1):
        cum = cum + sm[i:i + 1]
    return cum - first


def _log_forget_and_key(z, lb):
    t = jnp.exp(-jnp.abs(z))
    r = 1.0 / (1.0 + t)
    sig_neg = jnp.where(z >= 0.0, t * r, r)
    log_sig = jnp.minimum(z, 0.0) - jnp.log(1.0 + t)
    a = jnp.log(lb)
    b = jnp.log1p(-lb) + log_sig
    log_f = jnp.maximum(a, b) + jnp.log(1.0 + jnp.exp(-jnp.abs(a - b)))
    k = (1.0 - lb) * sig_neg
    return log_f, k


def _split3_bf16(x):
    hi = x.astype(BF16)
    r1 = x - hi.astype(F32)
    mid = r1.astype(BF16)
    lo = (r1 - mid.astype(F32)).astype(BF16)
    return hi, mid, lo


def _head_norm_gate(o, ng, gate):
    ms = jnp.mean(o * o, axis=-1, keepdims=True)
    return (o * lax.rsqrt(ms + EPS) * ng * gate).astype(BF16)


def _gla_prompt_kernel(q_ref, z_ref, v_ref, g_ref, lbr_ref, ng_ref, o_ref, s_ref,
                       st_ref, gh_ref, oacc_ref, x_ref, w_ref, ad_ref, *, layer, tl, nt, hg):
    C = min(GLA_CHUNK, tl)
    sub = min(GLA_SUB, C)
    n_sub = C // sub
    W = V7X_LANES
    t_idx = pl.program_id(2)

    @pl.when(t_idx == 0)
    def _():
        st_ref[...] = jnp.zeros(st_ref.shape, F32)
        r = lax.broadcasted_iota(jnp.int32, (sub * W, W), 0) // W
        c = lax.broadcasted_iota(jnp.int32, (sub * W, W), 1)
        for i in range(n_sub):
            w_ref[i] = (c == r + i * sub).astype(BF16)

    lb_all = _forget_lower_bound(lbr_ref[...], layer)
    tri = (lax.broadcasted_iota(jnp.int32, (C, C), 0)
           >= lax.broadcasted_iota(jnp.int32, (C, C), 1)).astype(BF16)
    a_row = lax.broadcasted_iota(jnp.int32, (sub, C), 0)
    a_col = lax.broadcasted_iota(jnp.int32, (sub, C), 1)
    zero_rows = jnp.zeros((V7X_SUBLANES, W), F32)

    n_chunks = tl // C

    def chunk_rows(ci):
        start = ci * C if isinstance(ci, int) else pl.multiple_of(ci * C, C)
        return pl.ds(start, C)

    def decay_terms(ci, slot):
        rows = chunk_rows(ci)
        log_f, k = _log_forget_and_key(z_ref[0, 0, rows, :], lb_all)
        hi, mid, lo = _split3_bf16(log_f)
        G = (_dot(tri, hi) + _dot(tri, mid) + _dot(tri, lo)) * LOG2E
        gh_ref[slot, 0] = G
        gh_ref[slot, 1] = G - jnp.log2(k)
        gh_ref[slot, 2] = k

    def dense(ci, slot):
        rows = chunk_rows(ci)
        for hh in range(hg):
            ls = slice(hh * W, (hh + 1) * W)
            q = q_ref[0, 0, rows, ls]
            v = v_ref[0, 0, rows, ls].astype(BF16)
            G = gh_ref[slot, 0, :, ls]
            k = gh_ref[slot, 2, :, ls]
            g_last = G[C - 1:C, :]
            st = st_ref[hh]
            o = _dot_nt((q * jnp.exp2(G)).astype(BF16), st.astype(BF16))
            off_blocks = [jnp.zeros((sub, C), BF16)]
            for i in range(n_sub):
                base = i * sub
                Gi, qi = G[base:base + sub], q[base:base + sub]
                for s in range(sub):
                    lo_row = (s // V7X_SUBLANES) * V7X_SUBLANES
                    h_s = gh_ref[slot, 1, base + s:base + s + 1, ls]
                    e = jnp.exp2(jnp.minimum(Gi[lo_row:] - h_s, 0.0)) * qi[lo_row:]
                    if lo_row:
                        e = jnp.concatenate([zero_rows] * (lo_row // V7X_SUBLANES) + [e], axis=0)
                    x_ref[i, hh * sub:(hh + 1) * sub, s * W:(s + 1) * W] = e.astype(BF16)
                if i > 0:
                    g_prev = gh_ref[slot, 0, base - 1:base, ls]
                    qt = (qi * jnp.exp2(Gi - g_prev)).astype(BF16)
                    kt = jnp.concatenate(
                        [(k[:base] * jnp.exp2(g_prev - G[:base])).astype(BF16),
                         jnp.zeros((C - base, W), BF16)], axis=0)
                    off_blocks.append(_dot_nt(qt, kt).astype(BF16))
            a_off = jnp.concatenate(off_blocks, axis=0) if n_sub > 1 else off_blocks[0]
            oacc_ref[hh] = o + _dot(a_off, v)
            kd = (k * jnp.exp2(g_last - G)).astype(BF16)
            st_ref[hh] = jnp.exp2(g_last) * st + _dot_tn(v, kd)

    def finish(ci):
        rows = chunk_rows(ci)
        for i in range(n_sub):
            ad_ref[i] = _dot(x_ref[i], w_ref[i])
        for hh in range(hg):
            ls = slice(hh * W, (hh + 1) * W)
            blocks = [jnp.where(a_col <= a_row + i * sub,
                                ad_ref[i, hh * sub:(hh + 1) * sub, 0:C], 0.0).astype(BF16)
                      for i in range(n_sub)]
            a_diag = jnp.concatenate(blocks, axis=0) if n_sub > 1 else blocks[0]
            o = oacc_ref[hh] + _dot(a_diag, v_ref[0, 0, rows, ls].astype(BF16))
            o_ref[0, rows, ls] = _head_norm_gate(o, ng_ref[:, ls], g_ref[0, 0, rows, ls])

    decay_terms(0, 0)

    def chunk(ci, carry):
        slot = ci % 2
        dense(ci, slot)
        finish(ci)
        decay_terms(jnp.minimum(ci + 1, n_chunks - 1), 1 - slot)
        return carry

    lax.fori_loop(0, n_chunks, chunk, 0)

    @pl.when(t_idx == nt - 1)
    def _():
        for hh in range(hg):
            s_ref[0, hh] = st_ref[hh].T


def _gla_prompt(P, lb_raw, layer, ng, n_heads):
    _, B, L, D = P.shape
    W = D // n_heads
    assert W == V7X_LANES
    hg = min(GLA_HEADS_PER_STEP, n_heads)
    tl = min(GLA_TL, L)
    nt = L // tl
    C = min(GLA_CHUNK, tl)
    sub = min(GLA_SUB, C)
    n_sub = C // sub
    NL = lb_raw.shape[0]
    seg = lambda s: pl.BlockSpec((1, 1, tl, hg * W), lambda b, h, t: (s, b, t, h))
    return pl.pallas_call(
        functools.partial(_gla_prompt_kernel, layer=layer, tl=tl, nt=nt, hg=hg),
        out_shape=(jax.ShapeDtypeStruct((B, L, D), BF16),
                   jax.ShapeDtypeStruct((B, n_heads, W, W), F32)),
        grid=(B, n_heads // hg, nt),
        in_specs=[seg(0), seg(1), seg(2), seg(3),
                  pl.BlockSpec((NL, hg * W), lambda b, h, t: (0, h)),
                  pl.BlockSpec((1, hg * W), lambda b, h, t: (0, h))],
        out_specs=(pl.BlockSpec((1, tl, hg * W), lambda b, h, t: (b, t, h)),
                   pl.BlockSpec((1, hg, W, W), lambda b, h, t: (b, h, 0, 0))),
        scratch_shapes=[pltpu.VMEM((hg, W, W), F32),
                        pltpu.VMEM((2, 3, C, hg * W), F32),
                        pltpu.VMEM((hg, C, W), F32),
                        pltpu.VMEM((n_sub, hg * sub, sub * W), BF16),
                        pltpu.VMEM((n_sub, sub * W, W), BF16),
                        pltpu.VMEM((n_sub, hg * sub, W), F32)],
        compiler_params=_params(3), name="gla_prompt",
    )(P, P, P, P, lb_raw, ng.reshape(1, D))


def _gla_sample_kernel(q_ref, z_ref, v_ref, g_ref, lbr_ref, ng_ref, s_ref, o_ref, ns_ref,
                       ft_ref, kt_ref, vb_ref, *, layer, bb):
    j = pl.program_id(1)
    NB = ft_ref.shape[1]

    @pl.when(j == 0)
    def _():
        lb = _forget_lower_bound(lbr_ref[...], layer)
        log_f, k = _log_forget_and_key(z_ref[0], lb)
        ft_ref[...] = jnp.exp(log_f).T
        kt_ref[...] = k.T
        vb_ref[...] = v_ref[0].astype(BF16)

    shift = (V7X_LANES - j * bb) % V7X_LANES
    ft = pltpu.roll(ft_ref[...], shift, axis=1)
    kt = kt_ref[...]
    vb = vb_ref[...]
    r0 = pl.multiple_of(j * bb, bb)
    qb = q_ref[0, pl.ds(r0, bb), :].astype(BF16)
    lane_b = lax.broadcasted_iota(jnp.int32, kt.shape, 1)
    row_b = lax.broadcasted_iota(jnp.int32, (bb, vb.shape[1]), 0)
    o_acc = jnp.zeros((bb, vb.shape[1]), F32)
    for i in range(bb):
        k_b = jnp.where(lane_b == r0 + i, kt, 0.0).astype(BF16)
        s_new = ft[:, i:i + 1] * s_ref[i, 0] + _dot(k_b, vb)
        ns_ref[i, 0] = s_new
        o_acc = jnp.where(row_b == i, _dot(qb, s_new.astype(BF16)), o_acc)
    o_ref[...] = _head_norm_gate(o_acc, ng_ref[...], g_ref[0, pl.ds(r0, bb), :])


def _gla_sample(P, state, lb_raw, layer, ng):
    _, NB, D = P.shape
    _, _, H, DK, DV = state.shape
    assert NB == V7X_LANES and DK == V7X_LANES and DV == V7X_LANES
    bb = 32
    NL = lb_raw.shape[0]
    seg = lambda s: pl.BlockSpec((1, NB, DK), lambda h, j: (s, 0, h))
    return pl.pallas_call(
        functools.partial(_gla_sample_kernel, layer=layer, bb=bb),
        out_shape=(jax.ShapeDtypeStruct((NB, D), BF16),
                   jax.ShapeDtypeStruct(state.shape[1:], F32)),
        grid=(H, NB // bb),
        in_specs=[seg(0), seg(1), seg(2), seg(3),
                  pl.BlockSpec((NL, DK), lambda h, j: (0, h)),
                  pl.BlockSpec((1, DV), lambda h, j: (0, h)),
                  pl.BlockSpec((None, bb, 1, DK, DV), lambda h, j: (layer, j, h, 0, 0))],
        out_specs=(pl.BlockSpec((bb, DV), lambda h, j: (j, h)),
                   pl.BlockSpec((bb, 1, DK, DV), lambda h, j: (j, h, 0, 0))),
        scratch_shapes=[pltpu.VMEM((DK, NB), F32), pltpu.VMEM((DK, NB), F32),
                        pltpu.VMEM((NB, DV), BF16)],
        compiler_params=_params(2), name="gla_sample",
    )(P, P, P, P, lb_raw, ng.reshape(1, D), state)


def kernel(x_prompt, x_sample, state_conv, state_hgrn, state_ffn, norm_mix, norm_ffn, norm_final,
           conv_w_pw1, conv_b_pw1, conv_w_dw, conv_b_dw, conv_ln_g, conv_ln_b, conv_w_pw2,
           conv_b_pw2, hgrn_w_q, hgrn_w_f, hgrn_w_i, hgrn_w_g, hgrn_w_o, hgrn_lb_raw,
           hgrn_norm_g, ffn_w_up, ffn_w_dw, ffn_b_dw, ffn_w_down):
    B, L, D = x_prompt.shape
    NB = x_sample.shape[0]
    depth = ffn_w_up.shape[0]
    n_heads = state_hgrn.shape[2]
    F = ffn_w_down.shape[1]
    lb_raw = hgrn_lb_raw.astype(F32)

    xp = x_prompt.reshape(B * L, D)
    xs = x_sample.reshape(NB, D)
    conv_p, hgrn_p, ffn_p, conv_s, hgrn_s, ffn_s = [], [], [], [], [], []
    acts = (True, False, False, True)
    w_pw1 = conv_w_pw1.astype(BF16)
    w_pw2 = conv_w_pw2.astype(BF16)
    w_o = hgrn_w_o.astype(BF16)
    w_up = ffn_w_up.astype(BF16)
    w_down = ffn_w_down.astype(BF16)
    prev_all = state_ffn.reshape(depth, NB, 2, 2, F).transpose(0, 2, 3, 1, 4)
    state_conv_t = state_conv.transpose(0, 2, 1, 3)
    for i in range(depth):
        j = i // 2
        if i % 2 == 0:
            cargs = (j, norm_mix[i], w_pw1, conv_b_pw1[j], conv_w_dw[j], conv_b_dw[j],
                     conv_ln_g[j], conv_ln_b[j], w_pw2, conv_b_pw2[j])
            xp, st = _conv_prompt(xp, B, L, *cargs)
            conv_p.append(st)
            xs, st = _conv_sample(xs, state_conv_t, *cargs)
            conv_s.append(st)
        else:
            w_stack = jnp.stack([hgrn_w_q[j], hgrn_w_f[j], hgrn_w_i[j], hgrn_w_g[j]]).astype(BF16)
            P = _proj(xp, norm_mix[i], w_stack, acts, tm=min(PROJ_TM, L))
            og, st = _gla_prompt(P.reshape(4, B, L, D), lb_raw, j, hgrn_norm_g[j], n_heads)
            hgrn_p.append(st)
            xp = _residual_matmul(og.reshape(B * L, D), w_o, j, xp, tm=min(PROJ_TM, L))
            P = _proj(xs, norm_mix[i], w_stack, acts, tm=NB)
            og, st = _gla_sample(P, state_hgrn, lb_raw, j, hgrn_norm_g[j])
            hgrn_s.append(st)
            xs = _residual_matmul(og, w_o, j, xs, tm=NB)
        last = i == depth - 1
        fargs = (i, norm_ffn[i], w_up, ffn_w_dw[i], ffn_b_dw[i], w_down, norm_final)
        xp, st = _ffn(xp, L, *fargs, final_norm=last)
        ffn_p.append(st.reshape(B, 2, 2 * F))
        prev = prev_all[i]
        xs, hn = _ffn(xs, 1, *fargs, prev=prev, final_norm=last)
        new_prev = jnp.stack([prev[1], hn], axis=0)
        ffn_s.append(new_prev.transpose(2, 0, 1, 3).reshape(NB, 2, 2 * F))
    return (xp.reshape(B, L, D), xs.reshape(NB, 1, D),
            jnp.stack(conv_p), jnp.stack(hgrn_p), jnp.stack(ffn_p),
            jnp.stack(conv_s).transpose(0, 2, 1, 3), jnp.stack(hgrn_s), jnp.stack(ffn_s))
```

```python
import functools

import jax
import jax.numpy as jnp
from jax import lax
from jax.experimental import pallas as pl
from jax.experimental.pallas import tpu as pltpu

F32 = jnp.float32
BF16 = jnp.bfloat16
EPS = 1e-6
LOG2E = 1.4426950408889634

V7X_VMEM_BYTES = 64 * 1024 * 1024
V7X_LANES = 128
V7X_SUBLANES = 8
V7X_MXU_DIM = 256
VMEM_LIMIT = V7X_VMEM_BYTES - 8 * 1024 * 1024

GLA_CHUNK = 64
GLA_SUB = 16
GLA_HEADS_PER_STEP = 8
GLA_TL = 1024
FFN_TM = 512
FFN_TF = 512
CONV_TL = 256
PROJ_TM = 512


def _params(n_axes, vmem=VMEM_LIMIT):
    return pltpu.CompilerParams(
        dimension_semantics=("arbitrary",) * n_axes, vmem_limit_bytes=vmem)


def _resident_layer(shape, layer):
    n = len(shape)
    return pl.BlockSpec((None,) + tuple(shape), lambda *_: (layer,) + (0,) * n,
                        pipeline_mode=pl.Buffered(1))


def _rms_bf16(x, w):
    ms = jnp.mean(x * x, axis=-1, keepdims=True)
    return (x * lax.rsqrt(ms + EPS) * w).astype(BF16)


def _silu(x):
    return x * jax.nn.sigmoid(x)


def _dot(a, b):
    return jnp.dot(a, b, preferred_element_type=F32)


def _dot_nt(a, b):
    return lax.dot_general(a, b, (((1,), (1,)), ((), ())), preferred_element_type=F32)


def _dot_tn(a, b):
    return lax.dot_general(a, b, (((0,), (0,)), ((), ())), preferred_element_type=F32)


def _ffn_kernel(*refs, prompt, tm, tiles_per_seq, final_norm, nf):
    if prompt:
        (x_ref, nw_ref, wg_ref, wu_ref, cw_ref, cb_ref, wd_ref, fw_ref,
         o_ref, st_ref, xn_ref, hs_ref, p_ref, carry_ref) = refs
    else:
        (x_ref, nw_ref, wg_ref, wu_ref, cw_ref, cb_ref, wd_ref, fw_ref, prev_ref,
         o_ref, hn_ref, xn_ref, hs_ref, p_ref) = refs
    m = pl.program_id(0)
    f = pl.program_id(1)
    S = V7X_SUBLANES
    tf = wg_ref.shape[1]

    def up_into(dst):
        xn = xn_ref[...]
        for half, w_ref in ((0, wg_ref), (1, wu_ref)):
            h = _dot(xn, w_ref[...])
            dst[half, S:, :] = h
            if not prompt:
                hn_ref[half] = h

    def conv_gate(src, fp):
        if prompt:
            for half in (0, 1):
                seq_start = (m % tiles_per_seq) == 0
                src[half, 0:S, :] = jnp.where(seq_start, 0.0, carry_ref[fp, half])
                carry_ref[fp, half] = src[half, tm:tm + S, :]
                cols = pl.ds(pl.multiple_of(fp * tf, tf), tf)
                for r in range(2):
                    st_ref[0, r, half:half + 1, cols] = src[half, S + tm - 2 + r:S + tm - 1 + r, :]
        rc = min(64, tm)
        cc = min(256, tf)
        for r0 in range(0, tm, rc):
            for c0 in range(0, tf, cc):
                cs = slice(c0, c0 + cc)
                ys = []
                for half in (0, 1):
                    h = src[half, S + r0:S + r0 + rc, cs]
                    if prompt:
                        h1 = src[half, S - 1 + r0:S - 1 + r0 + rc, cs]
                        h2 = src[half, S - 2 + r0:S - 2 + r0 + rc, cs]
                    else:
                        h2 = prev_ref[0, half, r0:r0 + rc, cs]
                        h1 = prev_ref[1, half, r0:r0 + rc, cs]
                    ys.append(cw_ref[2, half:half + 1, cs] * h + cw_ref[1, half:half + 1, cs] * h1
                              + cw_ref[0, half:half + 1, cs] * h2 + cb_ref[half:half + 1, cs])
                p_ref[r0:r0 + rc, cs] = (_silu(ys[0]) * ys[1]).astype(BF16)

    @pl.when(f == 0)
    def _():
        x = x_ref[...]
        xn_ref[...] = _rms_bf16(x, nw_ref[...])
        o_ref[...] = x

    up_into(hs_ref)
    conv_gate(hs_ref, f)
    o_ref[...] += _dot(p_ref[...], wd_ref[...])

    if final_norm:
        @pl.when(f == nf - 1)
        def _():
            o = o_ref[...]
            ms = jnp.mean(o * o, axis=-1, keepdims=True)
            o_ref[...] = o * lax.rsqrt(ms + EPS) * fw_ref[...]


def _ffn(x2d, seq_len, layer, nw, w_up, w_dw, b_dw, w_down, fw, prev=None, final_norm=False):
    T, D = x2d.shape
    F = w_down.shape[1]
    prompt = prev is None
    tm = min(FFN_TM, seq_len) if prompt else T
    tf = FFN_TF if F % FFN_TF == 0 else F
    nf = F // tf
    nm = T // tm
    cw = w_dw.reshape(3, 2, F)
    cb = b_dw.reshape(2, F)
    in_specs = [
        pl.BlockSpec((tm, D), lambda m, f: (m, 0)),
        pl.BlockSpec((1, D), lambda m, f: (0, 0)),
        pl.BlockSpec((None, D, tf), lambda m, f: (layer, 0, f)),
        pl.BlockSpec((None, D, tf), lambda m, f: (layer, 0, nf + f)),
        pl.BlockSpec((3, 2, tf), lambda m, f: (0, 0, f)),
        pl.BlockSpec((2, tf), lambda m, f: (0, f)),
        pl.BlockSpec((None, tf, D), lambda m, f: (layer, f, 0)),
        pl.BlockSpec((1, D), lambda m, f: (0, 0)),
    ]
    args = [x2d, nw.reshape(1, D), w_up, w_up, cw, cb, w_down, fw.reshape(1, D)]
    scratch = [pltpu.VMEM((tm, D), BF16), pltpu.VMEM((2, tm + V7X_SUBLANES, tf), F32),
               pltpu.VMEM((tm, tf), BF16)]
    if prompt:
        tiles_per_seq = seq_len // tm
        nb = T // seq_len
        out_shape = (jax.ShapeDtypeStruct((T, D), F32),
                     jax.ShapeDtypeStruct((nb, 2, 2, F), F32))
        out_specs = (pl.BlockSpec((tm, D), lambda m, f: (m, 0)),
                     pl.BlockSpec((1, 2, 2, F), lambda m, f: (m // tiles_per_seq, 0, 0, 0)))
        scratch += [pltpu.VMEM((nf, 2, V7X_SUBLANES, tf), F32)]
    else:
        tiles_per_seq = 1
        in_specs.append(pl.BlockSpec((2, 2, tm, tf), lambda m, f: (0, 0, m, f)))
        args.append(prev)
        out_shape = (jax.ShapeDtypeStruct((T, D), F32),
                     jax.ShapeDtypeStruct((2, T, F), F32))
        out_specs = (pl.BlockSpec((tm, D), lambda m, f: (m, 0)),
                     pl.BlockSpec((2, tm, tf), lambda m, f: (0, m, f)))
    kern = functools.partial(_ffn_kernel, prompt=prompt, tm=tm, tiles_per_seq=tiles_per_seq,
                             final_norm=final_norm, nf=nf)
    return pl.pallas_call(
        kern, out_shape=out_shape, grid=(nm, nf), in_specs=in_specs, out_specs=out_specs,
        scratch_shapes=scratch, compiler_params=_params(2),
        name="ffn_prompt" if prompt else "ffn_sample")(*args)


def _ln_silu_bf16(c, g, b):
    mu = jnp.mean(c, axis=-1, keepdims=True)
    d = c - mu
    var = jnp.mean(d * d, axis=-1, keepdims=True)
    return _silu(d * lax.rsqrt(var + EPS) * g + b).astype(BF16)


def _conv_prompt_kernel(x_ref, nw_ref, w1_ref, b1_ref, cw_ref, cb_ref, lg_ref, lb_ref,
                        w2_ref, b2_ref, o_ref, st_ref, us_ref, c_ref, cwb_ref,
                        *, tl, nt, dc, width):
    t = pl.program_id(1)
    S = V7X_SUBLANES
    halo = 32
    lead = halo - (width - 1)
    x = x_ref[...]
    xn = _rms_bf16(x, nw_ref[...])
    h = _dot(xn, w1_ref[...]) + b1_ref[...]
    u = h[:, :dc] * jax.nn.sigmoid(h[:, dc:])

    @pl.when(t == 0)
    def _():
        us_ref[0:halo, :] = jnp.zeros((halo, dc), F32)
        for j in range(width):
            cwb_ref[j] = jnp.broadcast_to(cw_ref[j:j + 1, :], (S, dc))

    us_ref[halo:, :] = u

    rb = min(128, tl)
    cwid = V7X_LANES

    def col_body(ci, carry):
        c0 = pl.multiple_of(ci * cwid, cwid)
        cols = pl.ds(c0, cwid)
        for r0 in range(0, tl, rb):
            acc = jnp.broadcast_to(cb_ref[:, cols], (rb // S, S, cwid))
            for a in range(S):
                offs = [lead + j for j in range(width) if (lead + j) % S == a]
                if not offs:
                    continue
                start = offs[0] - a
                span = -(-(rb + offs[-1] - start) // S) * S
                blk = us_ref[pl.ds(r0 + start, span), cols]
                if a:
                    blk = pltpu.roll(blk, span - a, axis=0)
                for off in offs:
                    d = off - offs[0]
                    acc = acc + cwb_ref[off - lead, :, cols] * blk[d:d + rb].reshape(rb // S, S, cwid)
            c_ref[pl.ds(r0, rb), cols] = acc.reshape(rb, cwid)
        return carry

    lax.fori_loop(0, dc // cwid, col_body, 0)

    @pl.when(t == nt - 1)
    def _():
        st_ref[0] = us_ref[halo + tl - (width - 1):halo + tl, :]

    us_ref[0:halo, :] = u[tl - halo:, :]

    c = _ln_silu_bf16(c_ref[...], lg_ref[...], lb_ref[...])
    o_ref[...] = x + _dot(c, w2_ref[...]) + b2_ref[...]


def _conv_prompt(x2d, nb, seq_len, layer, nw, w1, b1, cw, cb, lg, lb, w2, b2):
    T, D = x2d.shape
    dc = w2.shape[1]
    width = cw.shape[0]
    tl = min(CONV_TL, seq_len)
    nt = seq_len // tl
    row = lambda v: v.reshape(1, -1)
    kern = functools.partial(_conv_prompt_kernel, tl=tl, nt=nt, dc=dc, width=width)
    const2 = lambda shape: pl.BlockSpec(shape, lambda b, t: (0, 0))
    return pl.pallas_call(
        kern,
        out_shape=(jax.ShapeDtypeStruct((T, D), F32),
                   jax.ShapeDtypeStruct((nb, width - 1, dc), F32)),
        grid=(nb, nt),
        in_specs=[
            pl.BlockSpec((tl, D), lambda b, t: (b * nt + t, 0)),
            const2((1, D)),
            _resident_layer((D, 2 * dc), layer),
            const2((1, 2 * dc)),
            const2((width, dc)),
            const2((1, dc)), const2((1, dc)), const2((1, dc)),
            _resident_layer((dc, D), layer),
            const2((1, D)),
        ],
        out_specs=(pl.BlockSpec((tl, D), lambda b, t: (b * nt + t, 0)),
                   pl.BlockSpec((1, width - 1, dc), lambda b, t: (b, 0, 0))),
        scratch_shapes=[pltpu.VMEM((32 + tl, dc), F32), pltpu.VMEM((tl, dc), F32),
                        pltpu.VMEM((width, V7X_SUBLANES, dc), F32)],
        compiler_params=_params(2), name="conv_prompt",
    )(x2d, row(nw), w1, row(b1), cw, row(cb), row(lg), row(lb), w2, row(b2))


def _conv_sample_a_kernel(x_ref, nw_ref, wa_ref, wg_ref, ba_ref, bg_ref, cw_ref, cb_ref,
                          st_ref, c_ref, ns_ref, xn_ref, *, width):
    n = pl.program_id(0)

    @pl.when(n == 0)
    def _():
        xn_ref[...] = _rms_bf16(x_ref[...], nw_ref[...])

    xn = xn_ref[...]
    u = (_dot(xn, wa_ref[...]) + ba_ref[...]) * jax.nn.sigmoid(_dot(xn, wg_ref[...]) + bg_ref[...])
    acc = cb_ref[...] + cw_ref[width - 1:width, :] * u
    for j in range(width - 1):
        row = st_ref[j]
        acc = acc + cw_ref[j:j + 1, :] * row
        if j > 0:
            ns_ref[j - 1] = row
    ns_ref[width - 2] = u
    c_ref[...] = acc


def _conv_sample_b_kernel(c_ref, x_ref, lg_ref, lb_ref, w2_ref, b2_ref, o_ref, cn_ref):
    n = pl.program_id(0)

    @pl.when(n == 0)
    def _():
        cn_ref[...] = _ln_silu_bf16(c_ref[...], lg_ref[...], lb_ref[...])

    o_ref[...] = x_ref[...] + _dot(cn_ref[...], w2_ref[...]) + b2_ref[...]


def _conv_sample(x2d, state, layer, nw, w1, b1, cw, cb, lg, lb, w2, b2):
    NB, D = x2d.shape
    dc = w2.shape[1]
    width = cw.shape[0]
    tn = min(256, dc)
    nn = dc // tn
    row = lambda v: v.reshape(1, -1)
    c, new_state = pl.pallas_call(
        functools.partial(_conv_sample_a_kernel, width=width),
        out_shape=(jax.ShapeDtypeStruct((NB, dc), F32),
                   jax.ShapeDtypeStruct((width - 1, NB, dc), F32)),
        grid=(nn,),
        in_specs=[
            pl.BlockSpec((NB, D), lambda n: (0, 0)),
            pl.BlockSpec((1, D), lambda n: (0, 0)),
            pl.BlockSpec((None, D, tn), lambda n: (layer, 0, n)),
            pl.BlockSpec((None, D, tn), lambda n: (layer, 0, nn + n)),
            pl.BlockSpec((1, tn), lambda n: (0, n)),
            pl.BlockSpec((1, tn), lambda n: (0, nn + n)),
            pl.BlockSpec((width, tn), lambda n: (0, n)),
            pl.BlockSpec((1, tn), lambda n: (0, n)),
            pl.BlockSpec((None, width - 1, NB, tn), lambda n: (layer, 0, 0, n)),
        ],
        out_specs=(pl.BlockSpec((NB, tn), lambda n: (0, n)),
                   pl.BlockSpec((width - 1, NB, tn), lambda n: (0, 0, n))),
        scratch_shapes=[pltpu.VMEM((NB, D), BF16)],
        compiler_params=_params(1), name="conv_sample_a",
    )(x2d, row(nw), w1, w1, row(b1), row(b1), cw, row(cb), state)
    tn2 = min(512, D)
    y = pl.pallas_call(
        _conv_sample_b_kernel,
        out_shape=jax.ShapeDtypeStruct((NB, D), F32),
        grid=(D // tn2,),
        in_specs=[
            pl.BlockSpec((NB, dc), lambda n: (0, 0)),
            pl.BlockSpec((NB, tn2), lambda n: (0, n)),
            pl.BlockSpec((1, dc), lambda n: (0, 0)),
            pl.BlockSpec((1, dc), lambda n: (0, 0)),
            pl.BlockSpec((None, dc, tn2), lambda n: (layer, 0, n)),
            pl.BlockSpec((1, tn2), lambda n: (0, n)),
        ],
        out_specs=pl.BlockSpec((NB, tn2), lambda n: (0, n)),
        scratch_shapes=[pltpu.VMEM((NB, dc), BF16)],
        compiler_params=_params(1), name="conv_sample_b",
    )(c, x2d, row(lg), row(lb), w2, row(b2))
    return y, new_state


def _proj_kernel(x_ref, nw_ref, w_ref, o_ref, xn_ref, *, acts):
    s = pl.program_id(1)

    @pl.when(s == 0)
    def _():
        xn_ref[...] = _rms_bf16(x_ref[...], nw_ref[...])

    silu_segs = [i for i, a in enumerate(acts) if a]
    is_silu = functools.reduce(jnp.logical_or, [s == i for i in silu_segs])
    n = w_ref.shape[2]
    cw = 512 if n % 512 == 0 else n

    def segment(act):
        xn = xn_ref[...]
        for c0 in range(0, n, cw):
            y = _dot(xn, w_ref[0, :, c0:c0 + cw])
            o_ref[0, :, c0:c0 + cw] = _silu(y) if act else y

    @pl.when(is_silu)
    def _():
        segment(True)

    @pl.when(jnp.logical_not(is_silu))
    def _():
        segment(False)


def _proj(x2d, nw, w_stack, acts, tm):
    T, D = x2d.shape
    S, _, N = w_stack.shape
    return pl.pallas_call(
        functools.partial(_proj_kernel, acts=acts),
        out_shape=jax.ShapeDtypeStruct((S, T, N), F32),
        grid=(T // tm, S),
        in_specs=[pl.BlockSpec((tm, D), lambda m, s: (m, 0)),
                  pl.BlockSpec((1, D), lambda m, s: (0, 0)),
                  pl.BlockSpec((1, D, N), lambda m, s: (s, 0, 0))],
        out_specs=pl.BlockSpec((1, tm, N), lambda m, s: (s, m, 0)),
        scratch_shapes=[pltpu.VMEM((tm, D), BF16)],
        compiler_params=_params(2), name="hgrn_proj",
    )(x2d, nw.reshape(1, D), w_stack)


def _residual_matmul_kernel(a_ref, w_ref, x_ref, o_ref):
    o_ref[...] = x_ref[...] + _dot(a_ref[...], w_ref[...])


def _residual_matmul(a, w, layer, x2d, tm):
    T, K = a.shape
    N = w.shape[2]
    return pl.pallas_call(
        _residual_matmul_kernel,
        out_shape=jax.ShapeDtypeStruct((T, N), F32),
        grid=(T // tm,),
        in_specs=[pl.BlockSpec((tm, K), lambda m: (m, 0)),
                  _resident_layer((K, N), layer),
                  pl.BlockSpec((tm, N), lambda m: (m, 0))],
        out_specs=pl.BlockSpec((tm, N), lambda m: (m, 0)),
        compiler_params=_params(1), name="hgrn_out",
    )(a, w, x2d)


def _forget_lower_bound(raw, layer):
    mx = jnp.max(raw, axis=0, keepdims=True)
    e = jnp.exp(raw - mx)
    sm = e / jnp.sum(e, axis=0, keepdims=True)
    first = sm[0:1]
    cum = first
    for i in range(1, layer + 1):
        cum = cum + sm[i:i + 1]
    return cum - first


def _log_forget_and_key(z, lb):
    t = jnp.exp(-jnp.abs(z))
    r = 1.0 / (1.0 + t)
    sig_neg = jnp.where(z >= 0.0, t * r, r)
    log_sig = jnp.minimum(z, 0.0) - jnp.log(1.0 + t)
    a = jnp.log(lb)
    b = jnp.log1p(-lb) + log_sig
    log_f = jnp.maximum(a, b) + jnp.log(1.0 + jnp.exp(-jnp.abs(a - b)))
    k = (1.0 - lb) * sig_neg
    return log_f, k


def _split3_bf16(x):
    hi = x.astype(BF16)
    r1 = x - hi.astype(F32)
    mid = r1.astype(BF16)
    lo = (r1 - mid.astype(F32)).astype(BF16)
    return hi, mid, lo


def _head_norm_gate(o, ng, gate):
    ms = jnp.mean(o * o, axis=-1, keepdims=True)
    return (o * lax.rsqrt(ms + EPS) * ng * gate).astype(BF16)


def _gla_prompt_kernel(q_ref, z_ref, v_ref, g_ref, lbr_ref, ng_ref, o_ref, s_ref,
                       st_ref, gh_ref, oacc_ref, x_ref, w_ref, ad_ref, *, layer, tl, nt, hg):
    C = min(GLA_CHUNK, tl)
    sub = min(GLA_SUB, C)
    n_sub = C // sub
    W = V7X_LANES
    t_idx = pl.program_id(2)

    @pl.when(t_idx == 0)
    def _():
        st_ref[...] = jnp.zeros(st_ref.shape, F32)
        r = lax.broadcasted_iota(jnp.int32, (sub * W, W), 0) // W
        c = lax.broadcasted_iota(jnp.int32, (sub * W, W), 1)
        for i in range(n_sub):
            w_ref[i] = (c == r + i * sub).astype(BF16)

    lb_all = _forget_lower_bound(lbr_ref[...], layer)
    tri = (lax.broadcasted_iota(jnp.int32, (C, C), 0)
           >= lax.broadcasted_iota(jnp.int32, (C, C), 1)).astype(BF16)
    a_row = lax.broadcasted_iota(jnp.int32, (sub, C), 0)
    a_col = lax.broadcasted_iota(jnp.int32, (sub, C), 1)
    zero_rows = jnp.zeros((V7X_SUBLANES, W), F32)

    n_chunks = tl // C

    def chunk_rows(ci):
        start = ci * C if isinstance(ci, int) else pl.multiple_of(ci * C, C)
        return pl.ds(start, C)

    def decay_terms(ci, slot):
        rows = chunk_rows(ci)
        log_f, k = _log_forget_and_key(z_ref[0, 0, rows, :], lb_all)
        hi, mid, lo = _split3_bf16(log_f)
        G = (_dot(tri, hi) + _dot(tri, mid) + _dot(tri, lo)) * LOG2E
        gh_ref[slot, 0] = G
        gh_ref[slot, 1] = G - jnp.log2(k)
        gh_ref[slot, 2] = k

    def dense(ci, slot):
        rows = chunk_rows(ci)
        for hh in range(hg):
            ls = slice(hh * W, (hh + 1) * W)
            q = q_ref[0, 0, rows, ls]
            v = v_ref[0, 0, rows, ls].astype(BF16)
            G = gh_ref[slot, 0, :, ls]
            k = gh_ref[slot, 2, :, ls]
            g_last = G[C - 1:C, :]
            st = st_ref[hh]
            o = _dot_nt((q * jnp.exp2(G)).astype(BF16), st.astype(BF16))
            off_blocks = [jnp.zeros((sub, C), BF16)]
            for i in range(n_sub):
                base = i * sub
                Gi, qi = G[base:base + sub], q[base:base + sub]
                for s in range(sub):
                    lo_row = (s // V7X_SUBLANES) * V7X_SUBLANES
                    h_s = gh_ref[slot, 1, base + s:base + s + 1, ls]
                    e = jnp.exp2(jnp.minimum(Gi[lo_row:] - h_s, 0.0)) * qi[lo_row:]
                    if lo_row:
                        e = jnp.concatenate([zero_rows] * (lo_row // V7X_SUBLANES) + [e], axis=0)
                    x_ref[i, hh * sub:(hh + 1) * sub, s * W:(s + 1) * W] = e.astype(BF16)
                if i > 0:
                    g_prev = gh_ref[slot, 0, base - 1:base, ls]
                    qt = (qi * jnp.exp2(Gi - g_prev)).astype(BF16)
                    kt = jnp.concatenate(
                        [(k[:base] * jnp.exp2(g_prev - G[:base])).astype(BF16),
                         jnp.zeros((C - base, W), BF16)], axis=0)
                    off_blocks.append(_dot_nt(qt, kt).astype(BF16))
            a_off = jnp.concatenate(off_blocks, axis=0) if n_sub > 1 else off_blocks[0]
            oacc_ref[hh] = o + _dot(a_off, v)
            kd = (k * jnp.exp2(g_last - G)).astype(BF16)
            st_ref[hh] = jnp.exp2(g_last) * st + _dot_tn(v, kd)

    def finish(ci):
        rows = chunk_rows(ci)
        for i in range(n_sub):
            ad_ref[i] = _dot(x_ref[i], w_ref[i])
        for hh in range(hg):
            ls = slice(hh * W, (hh + 1) * W)
            blocks = [jnp.where(a_col <= a_row + i * sub,
                                ad_ref[i, hh * sub:(hh + 1) * sub, 0:C], 0.0).astype(BF16)
                      for i in range(n_sub)]
            a_diag = jnp.concatenate(blocks, axis=0) if n_sub > 1 else blocks[0]
            o = oacc_ref[hh] + _dot(a_diag, v_ref[0, 0, rows, ls].astype(BF16))
            o_ref[0, rows, ls] = _head_norm_gate(o, ng_ref[:, ls], g_ref[0, 0, rows, ls])

    decay_terms(0, 0)

    def chunk(ci, carry):
        slot = ci % 2
        dense(ci, slot)
        finish(ci)
        decay_terms(jnp.minimum(ci + 1, n_chunks - 1), 1 - slot)
        return carry

    lax.fori_loop(0, n_chunks, chunk, 0)

    @pl.when(t_idx == nt - 1)
    def _():
        for hh in range(hg):
            s_ref[0, hh] = st_ref[hh].T


def _gla_prompt(P, lb_raw, layer, ng, n_heads):
    _, B, L, D = P.shape
    W = D // n_heads
    assert W == V7X_LANES
    hg = min(GLA_HEADS_PER_STEP, n_heads)
    tl = min(GLA_TL, L)
    nt = L // tl
    C = min(GLA_CHUNK, tl)
    sub = min(GLA_SUB, C)
    n_sub = C // sub
    NL = lb_raw.shape[0]
    seg = lambda s: pl.BlockSpec((1, 1, tl, hg * W), lambda b, h, t: (s, b, t, h))
    return pl.pallas_call(
        functools.partial(_gla_prompt_kernel, layer=layer, tl=tl, nt=nt, hg=hg),
        out_shape=(jax.ShapeDtypeStruct((B, L, D), BF16),
                   jax.ShapeDtypeStruct((B, n_heads, W, W), F32)),
        grid=(B, n_heads // hg, nt),
        in_specs=[seg(0), seg(1), seg(2), seg(3),
                  pl.BlockSpec((NL, hg * W), lambda b, h, t: (0, h)),
                  pl.BlockSpec((1, hg * W), lambda b, h, t: (0, h))],
        out_specs=(pl.BlockSpec((1, tl, hg * W), lambda b, h, t: (b, t, h)),
                   pl.BlockSpec((1, hg, W, W), lambda b, h, t: (b, h, 0, 0))),
        scratch_shapes=[pltpu.VMEM((hg, W, W), F32),
                        pltpu.VMEM((2, 3, C, hg * W), F32),
                        pltpu.VMEM((hg, C, W), F32),
                        pltpu.VMEM((n_sub, hg * sub, sub * W), BF16),
                        pltpu.VMEM((n_sub, sub * W, W), BF16),
                        pltpu.VMEM((n_sub, hg * sub, W), F32)],
        compiler_params=_params(3), name="gla_prompt",
    )(P, P, P, P, lb_raw, ng.reshape(1, D))


def _gla_sample_kernel(*refs, layer, bb, n_earlier):
    q_ref, z_ref, v_ref, g_ref, lbr_ref, ng_ref, s_ref = refs[:7]
    earlier = refs[7:7 + n_earlier]
    o_ref, ns_all_ref, ft_ref, kt_ref, vb_ref = refs[7 + n_earlier:]
    if n_earlier:
        for l, e_ref in enumerate(earlier):
            ns_all_ref[l] = e_ref[...]
        ns_ref = ns_all_ref.at[n_earlier]
    else:
        ns_ref = ns_all_ref
    j = pl.program_id(1)

    @pl.when(j == 0)
    def _():
        lb = _forget_lower_bound(lbr_ref[...], layer)
        log_f, k = _log_forget_and_key(z_ref[0], lb)
        ft_ref[...] = jnp.exp(log_f).T
        kt_ref[...] = k.T
        vb_ref[...] = v_ref[0].astype(BF16)

    shift = (V7X_LANES - j * bb) % V7X_LANES
    ft = pltpu.roll(ft_ref[...], shift, axis=1)
    kt = kt_ref[...]
    vb = vb_ref[...]
    r0 = pl.multiple_of(j * bb, bb)
    qb = q_ref[0, pl.ds(r0, bb), :].astype(BF16)
    lane_b = lax.broadcasted_iota(jnp.int32, kt.shape, 1)
    row_b = lax.broadcasted_iota(jnp.int32, (bb, vb.shape[1]), 0)
    o_acc = jnp.zeros((bb, vb.shape[1]), F32)
    for i in range(bb):
        k_b = jnp.where(lane_b == r0 + i, kt, 0.0).astype(BF16)
        s_new = ft[:, i:i + 1] * s_ref[i, 0] + _dot(k_b, vb)
        ns_ref[i, 0] = s_new
        o_acc = jnp.where(row_b == i, _dot(qb, s_new.astype(BF16)), o_acc)
    o_ref[...] = _head_norm_gate(o_acc, ng_ref[...], g_ref[0, pl.ds(r0, bb), :])


def _gla_sample(P, state, lb_raw, layer, ng, earlier=()):
    _, NB, D = P.shape
    _, _, H, DK, DV = state.shape
    assert NB == V7X_LANES and DK == V7X_LANES and DV == V7X_LANES
    bb = 32
    NL = lb_raw.shape[0]
    n_earlier = len(earlier)
    seg = lambda s: pl.BlockSpec((1, NB, DK), lambda h, j: (s, 0, h))
    one_state = pl.BlockSpec((bb, 1, DK, DV), lambda h, j: (j, h, 0, 0))
    if n_earlier:
        st_shape = (n_earlier + 1,) + state.shape[1:]
        st_spec = pl.BlockSpec((n_earlier + 1, bb, 1, DK, DV), lambda h, j: (0, j, h, 0, 0))
    else:
        st_shape, st_spec = state.shape[1:], one_state
    return pl.pallas_call(
        functools.partial(_gla_sample_kernel, layer=layer, bb=bb, n_earlier=n_earlier),
        out_shape=(jax.ShapeDtypeStruct((NB, D), BF16), jax.ShapeDtypeStruct(st_shape, F32)),
        grid=(H, NB // bb),
        in_specs=[seg(0), seg(1), seg(2), seg(3),
                  pl.BlockSpec((NL, DK), lambda h, j: (0, h)),
                  pl.BlockSpec((1, DV), lambda h, j: (0, h)),
                  pl.BlockSpec((None, bb, 1, DK, DV), lambda h, j: (layer, j, h, 0, 0))]
                 + [one_state] * n_earlier,
        out_specs=(pl.BlockSpec((bb, DV), lambda h, j: (j, h)), st_spec),
        scratch_shapes=[pltpu.VMEM((DK, NB), F32), pltpu.VMEM((DK, NB), F32),
                        pltpu.VMEM((NB, DV), BF16)],
        compiler_params=_params(2), name="gla_sample",
    )(P, P, P, P, lb_raw, ng.reshape(1, D), state, *earlier)


def kernel(x_prompt, x_sample, state_conv, state_hgrn, state_ffn, norm_mix, norm_ffn, norm_final,
           conv_w_pw1, conv_b_pw1, conv_w_dw, conv_b_dw, conv_ln_g, conv_ln_b, conv_w_pw2,
           conv_b_pw2, hgrn_w_q, hgrn_w_f, hgrn_w_i, hgrn_w_g, hgrn_w_o, hgrn_lb_raw,
           hgrn_norm_g, ffn_w_up, ffn_w_dw, ffn_b_dw, ffn_w_down):
    B, L, D = x_prompt.shape
    NB = x_sample.shape[0]
    depth = ffn_w_up.shape[0]
    n_heads = state_hgrn.shape[2]
    F = ffn_w_down.shape[1]
    lb_raw = hgrn_lb_raw.astype(F32)

    xp = x_prompt.reshape(B * L, D)
    xs = x_sample.reshape(NB, D)
    conv_p, hgrn_p, ffn_p, conv_s, hgrn_s, ffn_s = [], [], [], [], [], []
    hgrn_s_all = None
    acts = (True, False, False, True)
    w_pw1 = conv_w_pw1.astype(BF16)
    w_pw2 = conv_w_pw2.astype(BF16)
    w_o = hgrn_w_o.astype(BF16)
    w_up = ffn_w_up.astype(BF16)
    w_down = ffn_w_down.astype(BF16)
    prev_all = state_ffn.reshape(depth, NB, 2, 2, F).transpose(0, 2, 3, 1, 4)
    state_conv_t = state_conv.transpose(0, 2, 1, 3)
    for i in range(depth):
        j = i // 2
        if i % 2 == 0:
            cargs = (j, norm_mix[i], w_pw1, conv_b_pw1[j], conv_w_dw[j], conv_b_dw[j],
                     conv_ln_g[j], conv_ln_b[j], w_pw2, conv_b_pw2[j])
            xp, st = _conv_prompt(xp, B, L, *cargs)
            conv_p.append(st)
            xs, st = _conv_sample(xs, state_conv_t, *cargs)
            conv_s.append(st)
        else:
            w_stack = jnp.stack([hgrn_w_q[j], hgrn_w_f[j], hgrn_w_i[j], hgrn_w_g[j]]).astype(BF16)
            P = _proj(xp, norm_mix[i], w_stack, acts, tm=min(PROJ_TM, L))
            og, st = _gla_prompt(P.reshape(4, B, L, D), lb_raw, j, hgrn_norm_g[j], n_heads)
            hgrn_p.append(st)
            xp = _residual_matmul(og.reshape(B * L, D), w_o, j, xp, tm=min(PROJ_TM, L))
            P = _proj(xs, norm_mix[i], w_stack, acts, tm=NB)
            if j == state_hgrn.shape[0] - 1 and hgrn_s:
                og, hgrn_s_all = _gla_sample(P, state_hgrn, lb_raw, j, hgrn_norm_g[j],
                                             earlier=tuple(hgrn_s))
            else:
                og, st = _gla_sample(P, state_hgrn, lb_raw, j, hgrn_norm_g[j])
                hgrn_s.append(st)
            xs = _residual_matmul(og, w_o, j, xs, tm=NB)
        last = i == depth - 1
        fargs = (i, norm_ffn[i], w_up, ffn_w_dw[i], ffn_b_dw[i], w_down, norm_final)
        xp, st = _ffn(xp, L, *fargs, final_norm=last)
        ffn_p.append(st.reshape(B, 2, 2 * F))
        prev = prev_all[i]
        xs, hn = _ffn(xs, 1, *fargs, prev=prev, final_norm=last)
        new_prev = jnp.stack([prev[1], hn], axis=0)
        ffn_s.append(new_prev.transpose(2, 0, 1, 3).reshape(NB, 2, 2 * F))
    return (xp.reshape(B, L, D), xs.reshape(NB, 1, D),
            jnp.stack(conv_p), jnp.stack(hgrn_p), jnp.stack(ffn_p),
            jnp.stack(conv_s).transpose(0, 2, 1, 3),
            hgrn_s_all if hgrn_s_all is not None else jnp.stack(hgrn_s), jnp.stack(ffn_s))
```

```python
import functools

import jax
import jax.numpy as jnp
from jax import lax
from jax.experimental import pallas as pl
from jax.experimental.pallas import tpu as pltpu

F32 = jnp.float32
BF16 = jnp.bfloat16
EPS = 1e-6
LOG2E = 1.4426950408889634

V7X_VMEM_BYTES = 64 * 1024 * 1024
V7X_LANES = 128
V7X_SUBLANES = 8
V7X_MXU_DIM = 256
VMEM_LIMIT = V7X_VMEM_BYTES - 8 * 1024 * 1024

GLA_CHUNK = 64
GLA_SUB = 16
GLA_HEADS_PER_STEP = 8
GLA_TL = 1024
FFN_TM = 512
FFN_TF = 512
CONV_TL = 256
PROJ_TM = 512


def _params(n_axes, vmem=VMEM_LIMIT):
    return pltpu.CompilerParams(
        dimension_semantics=("arbitrary",) * n_axes, vmem_limit_bytes=vmem)


def _resident_layer(shape, layer):
    n = len(shape)
    return pl.BlockSpec((None,) + tuple(shape), lambda *_: (layer,) + (0,) * n,
                        pipeline_mode=pl.Buffered(1))


def _rms_bf16(x, w):
    ms = jnp.mean(x * x, axis=-1, keepdims=True)
    return (x * lax.rsqrt(ms + EPS) * w).astype(BF16)


def _silu(x):
    return x * jax.nn.sigmoid(x)


def _dot(a, b):
    return jnp.dot(a, b, preferred_element_type=F32)


def _dot_nt(a, b):
    return lax.dot_general(a, b, (((1,), (1,)), ((), ())), preferred_element_type=F32)


def _dot_tn(a, b):
    return lax.dot_general(a, b, (((0,), (0,)), ((), ())), preferred_element_type=F32)


def _ffn_kernel(*refs, prompt, tm, tiles_per_seq, final_norm, nf):
    if prompt:
        (x_ref, nw_ref, wg_ref, wu_ref, cw_ref, cb_ref, wd_ref, fw_ref,
         o_ref, st_ref, xn_ref, hs_ref, p_ref, carry_ref) = refs
    else:
        (x_ref, nw_ref, wg_ref, wu_ref, cw_ref, cb_ref, wd_ref, fw_ref, pg_ref, pu_ref,
         o_ref, ns_ref, xn_ref, hs_ref, p_ref) = refs
        prev_refs = (pg_ref, pu_ref)
    m = pl.program_id(0)
    f = pl.program_id(1)
    S = V7X_SUBLANES
    tf = wg_ref.shape[1]

    def up_into(dst):
        xn = xn_ref[...]
        for half, w_ref in ((0, wg_ref), (1, wu_ref)):
            h = _dot(xn, w_ref[...])
            dst[half, S:, :] = h
            if not prompt:
                ns_ref[:, 0, half, :] = prev_refs[half][:, 1, :]
                ns_ref[:, 1, half, :] = h

    def conv_gate(src, fp):
        if prompt:
            for half in (0, 1):
                seq_start = (m % tiles_per_seq) == 0
                src[half, 0:S, :] = jnp.where(seq_start, 0.0, carry_ref[fp, half])
                carry_ref[fp, half] = src[half, tm:tm + S, :]
                cols = pl.ds(pl.multiple_of(fp * tf, tf), tf)
                for r in range(2):
                    st_ref[0, r, half:half + 1, cols] = src[half, S + tm - 2 + r:S + tm - 1 + r, :]
        rc = min(64, tm)
        cc = min(256, tf)
        for r0 in range(0, tm, rc):
            for c0 in range(0, tf, cc):
                cs = slice(c0, c0 + cc)
                ys = []
                for half in (0, 1):
                    h = src[half, S + r0:S + r0 + rc, cs]
                    if prompt:
                        h1 = src[half, S - 1 + r0:S - 1 + r0 + rc, cs]
                        h2 = src[half, S - 2 + r0:S - 2 + r0 + rc, cs]
                    else:
                        h2 = prev_refs[half][r0:r0 + rc, 0, cs]
                        h1 = prev_refs[half][r0:r0 + rc, 1, cs]
                    ys.append(cw_ref[2, half:half + 1, cs] * h + cw_ref[1, half:half + 1, cs] * h1
                              + cw_ref[0, half:half + 1, cs] * h2 + cb_ref[half:half + 1, cs])
                p_ref[r0:r0 + rc, cs] = (_silu(ys[0]) * ys[1]).astype(BF16)

    @pl.when(f == 0)
    def _():
        x = x_ref[...]
        xn_ref[...] = _rms_bf16(x, nw_ref[...])
        o_ref[...] = x

    up_into(hs_ref)
    conv_gate(hs_ref, f)
    o_ref[...] += _dot(p_ref[...], wd_ref[...])

    if final_norm:
        @pl.when(f == nf - 1)
        def _():
            o = o_ref[...]
            ms = jnp.mean(o * o, axis=-1, keepdims=True)
            o_ref[...] = o * lax.rsqrt(ms + EPS) * fw_ref[...]


def _ffn(x2d, seq_len, layer, nw, w_up, w_dw, b_dw, w_down, fw, prev=None, final_norm=False):
    T, D = x2d.shape
    F = w_down.shape[1]
    prompt = prev is None
    tm = min(FFN_TM, seq_len) if prompt else T
    tf = FFN_TF if F % FFN_TF == 0 else F
    nf = F // tf
    nm = T // tm
    cw = w_dw.reshape(3, 2, F)
    cb = b_dw.reshape(2, F)
    in_specs = [
        pl.BlockSpec((tm, D), lambda m, f: (m, 0)),
        pl.BlockSpec((1, D), lambda m, f: (0, 0)),
        pl.BlockSpec((None, D, tf), lambda m, f: (layer, 0, f)),
        pl.BlockSpec((None, D, tf), lambda m, f: (layer, 0, nf + f)),
        pl.BlockSpec((3, 2, tf), lambda m, f: (0, 0, f)),
        pl.BlockSpec((2, tf), lambda m, f: (0, f)),
        pl.BlockSpec((None, tf, D), lambda m, f: (layer, f, 0)),
        pl.BlockSpec((1, D), lambda m, f: (0, 0)),
    ]
    args = [x2d, nw.reshape(1, D), w_up, w_up, cw, cb, w_down, fw.reshape(1, D)]
    scratch = [pltpu.VMEM((tm, D), BF16), pltpu.VMEM((2, tm + V7X_SUBLANES, tf), F32),
               pltpu.VMEM((tm, tf), BF16)]
    if prompt:
        tiles_per_seq = seq_len // tm
        nb = T // seq_len
        out_shape = (jax.ShapeDtypeStruct((T, D), F32),
                     jax.ShapeDtypeStruct((nb, 2, 2, F), F32))
        out_specs = (pl.BlockSpec((tm, D), lambda m, f: (m, 0)),
                     pl.BlockSpec((1, 2, 2, F), lambda m, f: (m // tiles_per_seq, 0, 0, 0)))
        scratch += [pltpu.VMEM((nf, 2, V7X_SUBLANES, tf), F32)]
    else:
        tiles_per_seq = 1
        in_specs += [pl.BlockSpec((None, tm, 2, tf), lambda m, f: (layer, m, 0, f)),
                     pl.BlockSpec((None, tm, 2, tf), lambda m, f: (layer, m, 0, nf + f))]
        args += [prev, prev]
        out_shape = (jax.ShapeDtypeStruct((T, D), F32),
                     jax.ShapeDtypeStruct((T, 2, 2, F), F32))
        out_specs = (pl.BlockSpec((tm, D), lambda m, f: (m, 0)),
                     pl.BlockSpec((tm, 2, 2, tf), lambda m, f: (m, 0, 0, f)))
    kern = functools.partial(_ffn_kernel, prompt=prompt, tm=tm, tiles_per_seq=tiles_per_seq,
                             final_norm=final_norm, nf=nf)
    return pl.pallas_call(
        kern, out_shape=out_shape, grid=(nm, nf), in_specs=in_specs, out_specs=out_specs,
        scratch_shapes=scratch, compiler_params=_params(2),
        name="ffn_prompt" if prompt else "ffn_sample")(*args)


def _ln_silu_bf16(c, g, b):
    mu = jnp.mean(c, axis=-1, keepdims=True)
    d = c - mu
    var = jnp.mean(d * d, axis=-1, keepdims=True)
    return _silu(d * lax.rsqrt(var + EPS) * g + b).astype(BF16)


def _conv_prompt_kernel(x_ref, nw_ref, w1_ref, b1_ref, cw_ref, cb_ref, lg_ref, lb_ref,
                        w2_ref, b2_ref, o_ref, st_ref, us_ref, c_ref, cwb_ref,
                        *, tl, nt, dc, width):
    t = pl.program_id(1)
    S = V7X_SUBLANES
    halo = 32
    lead = halo - (width - 1)
    x = x_ref[...]
    xn = _rms_bf16(x, nw_ref[...])
    h = _dot(xn, w1_ref[...]) + b1_ref[...]
    u = h[:, :dc] * jax.nn.sigmoid(h[:, dc:])

    @pl.when(t == 0)
    def _():
        us_ref[0:halo, :] = jnp.zeros((halo, dc), F32)
        for j in range(width):
            cwb_ref[j] = jnp.broadcast_to(cw_ref[j:j + 1, :], (S, dc))

    us_ref[halo:, :] = u

    rb = min(128, tl)
    cwid = V7X_LANES

    def col_body(ci, carry):
        c0 = pl.multiple_of(ci * cwid, cwid)
        cols = pl.ds(c0, cwid)
        for r0 in range(0, tl, rb):
            acc = jnp.broadcast_to(cb_ref[:, cols], (rb // S, S, cwid))
            for a in range(S):
                offs = [lead + j for j in range(width) if (lead + j) % S == a]
                if not offs:
                    continue
                start = offs[0] - a
                span = -(-(rb + offs[-1] - start) // S) * S
                blk = us_ref[pl.ds(r0 + start, span), cols]
                if a:
                    blk = pltpu.roll(blk, span - a, axis=0)
                for off in offs:
                    d = off - offs[0]
                    acc = acc + cwb_ref[off - lead, :, cols] * blk[d:d + rb].reshape(rb // S, S, cwid)
            c_ref[pl.ds(r0, rb), cols] = acc.reshape(rb, cwid)
        return carry

    lax.fori_loop(0, dc // cwid, col_body, 0)

    @pl.when(t == nt - 1)
    def _():
        st_ref[0] = us_ref[halo + tl - (width - 1):halo + tl, :]

    us_ref[0:halo, :] = u[tl - halo:, :]

    c = _ln_silu_bf16(c_ref[...], lg_ref[...], lb_ref[...])
    o_ref[...] = x + _dot(c, w2_ref[...]) + b2_ref[...]


def _conv_prompt(x2d, nb, seq_len, layer, nw, w1, b1, cw, cb, lg, lb, w2, b2):
    T, D = x2d.shape
    dc = w2.shape[1]
    width = cw.shape[0]
    tl = min(CONV_TL, seq_len)
    nt = seq_len // tl
    row = lambda v: v.reshape(1, -1)
    kern = functools.partial(_conv_prompt_kernel, tl=tl, nt=nt, dc=dc, width=width)
    const2 = lambda shape: pl.BlockSpec(shape, lambda b, t: (0, 0))
    return pl.pallas_call(
        kern,
        out_shape=(jax.ShapeDtypeStruct((T, D), F32),
                   jax.ShapeDtypeStruct((nb, width - 1, dc), F32)),
        grid=(nb, nt),
        in_specs=[
            pl.BlockSpec((tl, D), lambda b, t: (b * nt + t, 0)),
            const2((1, D)),
            _resident_layer((D, 2 * dc), layer),
            const2((1, 2 * dc)),
            const2((width, dc)),
            const2((1, dc)), const2((1, dc)), const2((1, dc)),
            _resident_layer((dc, D), layer),
            const2((1, D)),
        ],
        out_specs=(pl.BlockSpec((tl, D), lambda b, t: (b * nt + t, 0)),
                   pl.BlockSpec((1, width - 1, dc), lambda b, t: (b, 0, 0))),
        scratch_shapes=[pltpu.VMEM((32 + tl, dc), F32), pltpu.VMEM((tl, dc), F32),
                        pltpu.VMEM((width, V7X_SUBLANES, dc), F32)],
        compiler_params=_params(2), name="conv_prompt",
    )(x2d, row(nw), w1, row(b1), cw, row(cb), row(lg), row(lb), w2, row(b2))


def _conv_sample_a_kernel(x_ref, nw_ref, wa_ref, wg_ref, ba_ref, bg_ref, cw_ref, cb_ref,
                          st_ref, c_ref, ns_ref, xn_ref, *, width):
    n = pl.program_id(0)

    @pl.when(n == 0)
    def _():
        xn_ref[...] = _rms_bf16(x_ref[...], nw_ref[...])

    xn = xn_ref[...]
    u = (_dot(xn, wa_ref[...]) + ba_ref[...]) * jax.nn.sigmoid(_dot(xn, wg_ref[...]) + bg_ref[...])
    acc = cb_ref[...] + cw_ref[width - 1:width, :] * u
    for j in range(width - 1):
        row = st_ref[j]
        acc = acc + cw_ref[j:j + 1, :] * row
        if j > 0:
            ns_ref[j - 1] = row
    ns_ref[width - 2] = u
    c_ref[...] = acc


def _conv_sample_b_kernel(c_ref, x_ref, lg_ref, lb_ref, w2_ref, b2_ref, o_ref, cn_ref):
    n = pl.program_id(0)

    @pl.when(n == 0)
    def _():
        cn_ref[...] = _ln_silu_bf16(c_ref[...], lg_ref[...], lb_ref[...])

    o_ref[...] = x_ref[...] + _dot(cn_ref[...], w2_ref[...]) + b2_ref[...]


def _conv_sample(x2d, state, layer, nw, w1, b1, cw, cb, lg, lb, w2, b2):
    NB, D = x2d.shape
    dc = w2.shape[1]
    width = cw.shape[0]
    tn = min(256, dc)
    nn = dc // tn
    row = lambda v: v.reshape(1, -1)
    c, new_state = pl.pallas_call(
        functools.partial(_conv_sample_a_kernel, width=width),
        out_shape=(jax.ShapeDtypeStruct((NB, dc), F32),
                   jax.ShapeDtypeStruct((width - 1, NB, dc), F32)),
        grid=(nn,),
        in_specs=[
            pl.BlockSpec((NB, D), lambda n: (0, 0)),
            pl.BlockSpec((1, D), lambda n: (0, 0)),
            pl.BlockSpec((None, D, tn), lambda n: (layer, 0, n)),
            pl.BlockSpec((None, D, tn), lambda n: (layer, 0, nn + n)),
            pl.BlockSpec((1, tn), lambda n: (0, n)),
            pl.BlockSpec((1, tn), lambda n: (0, nn + n)),
            pl.BlockSpec((width, tn), lambda n: (0, n)),
            pl.BlockSpec((1, tn), lambda n: (0, n)),
            pl.BlockSpec((None, width - 1, NB, tn), lambda n: (layer, 0, 0, n)),
        ],
        out_specs=(pl.BlockSpec((NB, tn), lambda n: (0, n)),
                   pl.BlockSpec((width - 1, NB, tn), lambda n: (0, 0, n))),
        scratch_shapes=[pltpu.VMEM((NB, D), BF16)],
        compiler_params=_params(1), name="conv_sample_a",
    )(x2d, row(nw), w1, w1, row(b1), row(b1), cw, row(cb), state)
    tn2 = min(512, D)
    y = pl.pallas_call(
        _conv_sample_b_kernel,
        out_shape=jax.ShapeDtypeStruct((NB, D), F32),
        grid=(D // tn2,),
        in_specs=[
            pl.BlockSpec((NB, dc), lambda n: (0, 0)),
            pl.BlockSpec((NB, tn2), lambda n: (0, n)),
            pl.BlockSpec((1, dc), lambda n: (0, 0)),
            pl.BlockSpec((1, dc), lambda n: (0, 0)),
            pl.BlockSpec((None, dc, tn2), lambda n: (layer, 0, n)),
            pl.BlockSpec((1, tn2), lambda n: (0, n)),
        ],
        out_specs=pl.BlockSpec((NB, tn2), lambda n: (0, n)),
        scratch_shapes=[pltpu.VMEM((NB, dc), BF16)],
        compiler_params=_params(1), name="conv_sample_b",
    )(c, x2d, row(lg), row(lb), w2, row(b2))
    return y, new_state


def _proj_kernel(x_ref, nw_ref, w_ref, o_ref, xn_ref, *, acts):
    s = pl.program_id(1)

    @pl.when(s == 0)
    def _():
        xn_ref[...] = _rms_bf16(x_ref[...], nw_ref[...])

    silu_segs = [i for i, a in enumerate(acts) if a]
    is_silu = functools.reduce(jnp.logical_or, [s == i for i in silu_segs])
    n = w_ref.shape[2]
    cw = 512 if n % 512 == 0 else n

    def segment(act):
        xn = xn_ref[...]
        for c0 in range(0, n, cw):
            y = _dot(xn, w_ref[0, :, c0:c0 + cw])
            o_ref[0, :, c0:c0 + cw] = _silu(y) if act else y

    @pl.when(is_silu)
    def _():
        segment(True)

    @pl.when(jnp.logical_not(is_silu))
    def _():
        segment(False)


def _proj(x2d, nw, w_stack, acts, tm):
    T, D = x2d.shape
    S, _, N = w_stack.shape
    return pl.pallas_call(
        functools.partial(_proj_kernel, acts=acts),
        out_shape=jax.ShapeDtypeStruct((S, T, N), F32),
        grid=(T // tm, S),
        in_specs=[pl.BlockSpec((tm, D), lambda m, s: (m, 0)),
                  pl.BlockSpec((1, D), lambda m, s: (0, 0)),
                  pl.BlockSpec((1, D, N), lambda m, s: (s, 0, 0))],
        out_specs=pl.BlockSpec((1, tm, N), lambda m, s: (s, m, 0)),
        scratch_shapes=[pltpu.VMEM((tm, D), BF16)],
        compiler_params=_params(2), name="hgrn_proj",
    )(x2d, nw.reshape(1, D), w_stack)


def _residual_matmul_kernel(a_ref, w_ref, x_ref, o_ref):
    o_ref[...] = x_ref[...] + _dot(a_ref[...], w_ref[...])


def _residual_matmul(a, w, layer, x2d, tm):
    T, K = a.shape
    N = w.shape[2]
    return pl.pallas_call(
        _residual_matmul_kernel,
        out_shape=jax.ShapeDtypeStruct((T, N), F32),
        grid=(T // tm,),
        in_specs=[pl.BlockSpec((tm, K), lambda m: (m, 0)),
                  _resident_layer((K, N), layer),
                  pl.BlockSpec((tm, N), lambda m: (m, 0))],
        out_specs=pl.BlockSpec((tm, N), lambda m: (m, 0)),
        compiler_params=_params(1), name="hgrn_out",
    )(a, w, x2d)


def _forget_lower_bound(raw, layer):
    mx = jnp.max(raw, axis=0, keepdims=True)
    e = jnp.exp(raw - mx)
    sm = e / jnp.sum(e, axis=0, keepdims=True)
    first = sm[0:1]
    cum = first
    for i in range(1, layer + 1):
        cum = cum + sm[i:i + 1]
    return cum - first


def _log_forget_and_key(z, lb):
    t = jnp.exp(-jnp.abs(z))
    r = 1.0 / (1.0 + t)
    sig_neg = jnp.where(z >= 0.0, t * r, r)
    log_sig = jnp.minimum(z, 0.0) - jnp.log(1.0 + t)
    a = jnp.log(lb)
    b = jnp.log1p(-lb) + log_sig
    log_f = jnp.maximum(a, b) + jnp.log(1.0 + jnp.exp(-jnp.abs(a - b)))
    k = (1.0 - lb) * sig_neg
    return log_f, k


def _split3_bf16(x):
    hi = x.astype(BF16)
    r1 = x - hi.astype(F32)
    mid = r1.astype(BF16)
    lo = (r1 - mid.astype(F32)).astype(BF16)
    return hi, mid, lo


def _head_norm_gate(o, ng, gate):
    ms = jnp.mean(o * o, axis=-1, keepdims=True)
    return (o * lax.rsqrt(ms + EPS) * ng * gate).astype(BF16)


def _gla_prompt_kernel(q_ref, z_ref, v_ref, g_ref, lbr_ref, ng_ref, o_ref, s_ref,
                       st_ref, gh_ref, oacc_ref, x_ref, w_ref, ad_ref, *, layer, tl, nt, hg):
    C = min(GLA_CHUNK, tl)
    sub = min(GLA_SUB, C)
    n_sub = C // sub
    W = V7X_LANES
    t_idx = pl.program_id(2)

    @pl.when(t_idx == 0)
    def _():
        st_ref[...] = jnp.zeros(st_ref.shape, F32)
        r = lax.broadcasted_iota(jnp.int32, (sub * W, W), 0) // W
        c = lax.broadcasted_iota(jnp.int32, (sub * W, W), 1)
        for i in range(n_sub):
            w_ref[i] = (c == r + i * sub).astype(BF16)

    lb_all = _forget_lower_bound(lbr_ref[...], layer)
    tri = (lax.broadcasted_iota(jnp.int32, (C, C), 0)
           >= lax.broadcasted_iota(jnp.int32, (C, C), 1)).astype(BF16)
    a_row = lax.broadcasted_iota(jnp.int32, (sub, C), 0)
    a_col = lax.broadcasted_iota(jnp.int32, (sub, C), 1)
    zero_rows = jnp.zeros((V7X_SUBLANES, W), F32)

    n_chunks = tl // C

    def chunk_rows(ci):
        start = ci * C if isinstance(ci, int) else pl.multiple_of(ci * C, C)
        return pl.ds(start, C)

    def decay_terms(ci, slot):
        rows = chunk_rows(ci)
        log_f, k = _log_forget_and_key(z_ref[0, 0, rows, :], lb_all)
        hi, mid, lo = _split3_bf16(log_f)
        G = (_dot(tri, hi) + _dot(tri, mid) + _dot(tri, lo)) * LOG2E
        gh_ref[slot, 0] = G
        gh_ref[slot, 1] = G - jnp.log2(k)
        gh_ref[slot, 2] = k

    def dense(ci, slot):
        rows = chunk_rows(ci)
        for hh in range(hg):
            ls = slice(hh * W, (hh + 1) * W)
            q = q_ref[0, 0, rows, ls]
            v = v_ref[0, 0, rows, ls].astype(BF16)
            G = gh_ref[slot, 0, :, ls]
            k = gh_ref[slot, 2, :, ls]
            g_last = G[C - 1:C, :]
            st = st_ref[hh]
            o = _dot_nt((q * jnp.exp2(G)).astype(BF16), st.astype(BF16))
            off_blocks = [jnp.zeros((sub, C), BF16)]
            for i in range(n_sub):
                base = i * sub
                Gi, qi = G[base:base + sub], q[base:base + sub]
                for s in range(sub):
                    lo_row = (s // V7X_SUBLANES) * V7X_SUBLANES
                    h_s = gh_ref[slot, 1, base + s:base + s + 1, ls]
                    e = jnp.exp2(jnp.minimum(Gi[lo_row:] - h_s, 0.0)) * qi[lo_row:]
                    if lo_row:
                        e = jnp.concatenate([zero_rows] * (lo_row // V7X_SUBLANES) + [e], axis=0)
                    x_ref[i, hh * sub:(hh + 1) * sub, s * W:(s + 1) * W] = e.astype(BF16)
                if i > 0:
                    g_prev = gh_ref[slot, 0, base - 1:base, ls]
                    qt = (qi * jnp.exp2(Gi - g_prev)).astype(BF16)
                    kt = jnp.concatenate(
                        [(k[:base] * jnp.exp2(g_prev - G[:base])).astype(BF16),
                         jnp.zeros((C - base, W), BF16)], axis=0)
                    off_blocks.append(_dot_nt(qt, kt).astype(BF16))
            a_off = jnp.concatenate(off_blocks, axis=0) if n_sub > 1 else off_blocks[0]
            oacc_ref[hh] = o + _dot(a_off, v)
            kd = (k * jnp.exp2(g_last - G)).astype(BF16)
            st_ref[hh] = jnp.exp2(g_last) * st + _dot_tn(v, kd)

    def finish(ci):
        rows = chunk_rows(ci)
        for i in range(n_sub):
            ad_ref[i] = _dot(x_ref[i], w_ref[i])
        for hh in range(hg):
            ls = slice(hh * W, (hh + 1) * W)
            blocks = [jnp.where(a_col <= a_row + i * sub,
                                ad_ref[i, hh * sub:(hh + 1) * sub, 0:C], 0.0).astype(BF16)
                      for i in range(n_sub)]
            a_diag = jnp.concatenate(blocks, axis=0) if n_sub > 1 else blocks[0]
            o = oacc_ref[hh] + _dot(a_diag, v_ref[0, 0, rows, ls].astype(BF16))
            o_ref[0, rows, ls] = _head_norm_gate(o, ng_ref[:, ls], g_ref[0, 0, rows, ls])

    decay_terms(0, 0)

    def chunk(ci, carry):
        slot = ci % 2
        dense(ci, slot)
        finish(ci)
        decay_terms(jnp.minimum(ci + 1, n_chunks - 1), 1 - slot)
        return carry

    lax.fori_loop(0, n_chunks, chunk, 0)

    @pl.when(t_idx == nt - 1)
    def _():
        for hh in range(hg):
            s_ref[0, hh] = st_ref[hh].T


def _gla_prompt(P, lb_raw, layer, ng, n_heads):
    _, B, L, D = P.shape
    W = D // n_heads
    assert W == V7X_LANES
    hg = min(GLA_HEADS_PER_STEP, n_heads)
    tl = min(GLA_TL, L)
    nt = L // tl
    C = min(GLA_CHUNK, tl)
    sub = min(GLA_SUB, C)
    n_sub = C // sub
    NL = lb_raw.shape[0]
    seg = lambda s: pl.BlockSpec((1, 1, tl, hg * W), lambda b, h, t: (s, b, t, h))
    return pl.pallas_call(
        functools.partial(_gla_prompt_kernel, layer=layer, tl=tl, nt=nt, hg=hg),
        out_shape=(jax.ShapeDtypeStruct((B, L, D), BF16),
                   jax.ShapeDtypeStruct((B, n_heads, W, W), F32)),
        grid=(B, n_heads // hg, nt),
        in_specs=[seg(0), seg(1), seg(2), seg(3),
                  pl.BlockSpec((NL, hg * W), lambda b, h, t: (0, h)),
                  pl.BlockSpec((1, hg * W), lambda b, h, t: (0, h))],
        out_specs=(pl.BlockSpec((1, tl, hg * W), lambda b, h, t: (b, t, h)),
                   pl.BlockSpec((1, hg, W, W), lambda b, h, t: (b, h, 0, 0))),
        scratch_shapes=[pltpu.VMEM((hg, W, W), F32),
                        pltpu.VMEM((2, 3, C, hg * W), F32),
                        pltpu.VMEM((hg, C, W), F32),
                        pltpu.VMEM((n_sub, hg * sub, sub * W), BF16),
                        pltpu.VMEM((n_sub, sub * W, W), BF16),
                        pltpu.VMEM((n_sub, hg * sub, W), F32)],
        compiler_params=_params(3), name="gla_prompt",
    )(P, P, P, P, lb_raw, ng.reshape(1, D))


def _gla_sample_kernel(*refs, layer, bb, n_earlier):
    q_ref, z_ref, v_ref, g_ref, lbr_ref, ng_ref, s_ref = refs[:7]
    earlier = refs[7:7 + n_earlier]
    o_ref, ns_all_ref, ft_ref, kt_ref, vb_ref = refs[7 + n_earlier:]
    if n_earlier:
        for l, e_ref in enumerate(earlier):
            ns_all_ref[l] = e_ref[...]
        ns_ref = ns_all_ref.at[n_earlier]
    else:
        ns_ref = ns_all_ref
    j = pl.program_id(1)

    @pl.when(j == 0)
    def _():
        lb = _forget_lower_bound(lbr_ref[...], layer)
        log_f, k = _log_forget_and_key(z_ref[0], lb)
        ft_ref[...] = jnp.exp(log_f).T
        kt_ref[...] = k.T
        vb_ref[...] = v_ref[0].astype(BF16)

    shift = (V7X_LANES - j * bb) % V7X_LANES
    ft = pltpu.roll(ft_ref[...], shift, axis=1)
    kt = kt_ref[...]
    vb = vb_ref[...]
    r0 = pl.multiple_of(j * bb, bb)
    qb = q_ref[0, pl.ds(r0, bb), :].astype(BF16)
    lane_b = lax.broadcasted_iota(jnp.int32, kt.shape, 1)
    row_b = lax.broadcasted_iota(jnp.int32, (bb, vb.shape[1]), 0)
    o_acc = jnp.zeros((bb, vb.shape[1]), F32)
    for i in range(bb):
        k_b = jnp.where(lane_b == r0 + i, kt, 0.0).astype(BF16)
        s_new = ft[:, i:i + 1] * s_ref[i, 0] + _dot(k_b, vb)
        ns_ref[i, 0] = s_new
        o_acc = jnp.where(row_b == i, _dot(qb, s_new.astype(BF16)), o_acc)
    o_ref[...] = _head_norm_gate(o_acc, ng_ref[...], g_ref[0, pl.ds(r0, bb), :])


def _gla_sample(P, state, lb_raw, layer, ng, earlier=()):
    _, NB, D = P.shape
    _, _, H, DK, DV = state.shape
    assert NB == V7X_LANES and DK == V7X_LANES and DV == V7X_LANES
    bb = 32
    NL = lb_raw.shape[0]
    n_earlier = len(earlier)
    seg = lambda s: pl.BlockSpec((1, NB, DK), lambda h, j: (s, 0, h))
    one_state = pl.BlockSpec((bb, 1, DK, DV), lambda h, j: (j, h, 0, 0))
    if n_earlier:
        st_shape = (n_earlier + 1,) + state.shape[1:]
        st_spec = pl.BlockSpec((n_earlier + 1, bb, 1, DK, DV), lambda h, j: (0, j, h, 0, 0))
    else:
        st_shape, st_spec = state.shape[1:], one_state
    return pl.pallas_call(
        functools.partial(_gla_sample_kernel, layer=layer, bb=bb, n_earlier=n_earlier),
        out_shape=(jax.ShapeDtypeStruct((NB, D), BF16), jax.ShapeDtypeStruct(st_shape, F32)),
        grid=(H, NB // bb),
        in_specs=[seg(0), seg(1), seg(2), seg(3),
                  pl.BlockSpec((NL, DK), lambda h, j: (0, h)),
                  pl.BlockSpec((1, DV), lambda h, j: (0, h)),
                  pl.BlockSpec((None, bb, 1, DK, DV), lambda h, j: (layer, j, h, 0, 0))]
                 + [one_state] * n_earlier,
        out_specs=(pl.BlockSpec((bb, DV), lambda h, j: (j, h)), st_spec),
        scratch_shapes=[pltpu.VMEM((DK, NB), F32), pltpu.VMEM((DK, NB), F32),
                        pltpu.VMEM((NB, DV), BF16)],
        compiler_params=_params(2), name="gla_sample",
    )(P, P, P, P, lb_raw, ng.reshape(1, D), state, *earlier)


def kernel(x_prompt, x_sample, state_conv, state_hgrn, state_ffn, norm_mix, norm_ffn, norm_final,
           conv_w_pw1, conv_b_pw1, conv_w_dw, conv_b_dw, conv_ln_g, conv_ln_b, conv_w_pw2,
           conv_b_pw2, hgrn_w_q, hgrn_w_f, hgrn_w_i, hgrn_w_g, hgrn_w_o, hgrn_lb_raw,
           hgrn_norm_g, ffn_w_up, ffn_w_dw, ffn_b_dw, ffn_w_down):
    B, L, D = x_prompt.shape
    NB = x_sample.shape[0]
    depth = ffn_w_up.shape[0]
    n_heads = state_hgrn.shape[2]
    F = ffn_w_down.shape[1]
    lb_raw = hgrn_lb_raw.astype(F32)

    xp = x_prompt.reshape(B * L, D)
    xs = x_sample.reshape(NB, D)
    conv_p, hgrn_p, ffn_p, conv_s, hgrn_s, ffn_s = [], [], [], [], [], []
    hgrn_s_all = None
    acts = (True, False, False, True)
    w_pw1 = conv_w_pw1.astype(BF16)
    w_pw2 = conv_w_pw2.astype(BF16)
    w_o = hgrn_w_o.astype(BF16)
    w_up = ffn_w_up.astype(BF16)
    w_down = ffn_w_down.astype(BF16)
    state_conv_t = state_conv.transpose(0, 2, 1, 3)
    for i in range(depth):
        j = i // 2
        if i % 2 == 0:
            cargs = (j, norm_mix[i], w_pw1, conv_b_pw1[j], conv_w_dw[j], conv_b_dw[j],
                     conv_ln_g[j], conv_ln_b[j], w_pw2, conv_b_pw2[j])
            xp, st = _conv_prompt(xp, B, L, *cargs)
            conv_p.append(st)
            xs, st = _conv_sample(xs, state_conv_t, *cargs)
            conv_s.append(st)
        else:
            w_stack = jnp.stack([hgrn_w_q[j], hgrn_w_f[j], hgrn_w_i[j], hgrn_w_g[j]]).astype(BF16)
            P = _proj(xp, norm_mix[i], w_stack, acts, tm=min(PROJ_TM, L))
            og, st = _gla_prompt(P.reshape(4, B, L, D), lb_raw, j, hgrn_norm_g[j], n_heads)
            hgrn_p.append(st)
            xp = _residual_matmul(og.reshape(B * L, D), w_o, j, xp, tm=min(PROJ_TM, L))
            P = _proj(xs, norm_mix[i], w_stack, acts, tm=NB)
            if j == state_hgrn.shape[0] - 1 and hgrn_s:
                og, hgrn_s_all = _gla_sample(P, state_hgrn, lb_raw, j, hgrn_norm_g[j],
                                             earlier=tuple(hgrn_s))
            else:
                og, st = _gla_sample(P, state_hgrn, lb_raw, j, hgrn_norm_g[j])
                hgrn_s.append(st)
            xs = _residual_matmul(og, w_o, j, xs, tm=NB)
        last = i == depth - 1
        fargs = (i, norm_ffn[i], w_up, ffn_w_dw[i], ffn_b_dw[i], w_down, norm_final)
        xp, st = _ffn(xp, L, *fargs, final_norm=last)
        ffn_p.append(st.reshape(B, 2, 2 * F))
        xs, st = _ffn(xs, 1, *fargs, prev=state_ffn, final_norm=last)
        ffn_s.append(st.reshape(NB, 2, 2 * F))
    return (xp.reshape(B, L, D), xs.reshape(NB, 1, D),
            jnp.stack(conv_p), jnp.stack(hgrn_p), jnp.stack(ffn_p),
            jnp.stack(conv_s).transpose(0, 2, 1, 3),
            hgrn_s_all if hgrn_s_all is not None else jnp.stack(hgrn_s), jnp.stack(ffn_s))
```

```python
import functools

import jax
import jax.numpy as jnp
from jax import lax
from jax.experimental import pallas as pl
from jax.experimental.pallas import tpu as pltpu

F32 = jnp.float32
BF16 = jnp.bfloat16
EPS = 1e-6
LOG2E = 1.4426950408889634

V7X_VMEM_BYTES = 64 * 1024 * 1024
V7X_LANES = 128
V7X_SUBLANES = 8
V7X_MXU_DIM = 256
VMEM_LIMIT = V7X_VMEM_BYTES - 8 * 1024 * 1024

GLA_CHUNK = 64
GLA_SUB = 16
GLA_HEADS_PER_STEP = 8
GLA_TL = 1024
FFN_TM = 512
FFN_TF = 512
CONV_TL = 256
PROJ_TM = 512


def _params(n_axes, vmem=VMEM_LIMIT):
    return pltpu.CompilerParams(
        dimension_semantics=("arbitrary",) * n_axes, vmem_limit_bytes=vmem)


def _resident_layer(shape, layer):
    n = len(shape)
    return pl.BlockSpec((None,) + tuple(shape), lambda *_: (layer,) + (0,) * n,
                        pipeline_mode=pl.Buffered(1))


def _rms_bf16(x, w):
    ms = jnp.mean(x * x, axis=-1, keepdims=True)
    return (x * lax.rsqrt(ms + EPS) * w).astype(BF16)


def _silu(x):
    return x * jax.nn.sigmoid(x)


def _dot(a, b):
    return jnp.dot(a, b, preferred_element_type=F32)


def _dot_nt(a, b):
    return lax.dot_general(a, b, (((1,), (1,)), ((), ())), preferred_element_type=F32)


def _dot_tn(a, b):
    return lax.dot_general(a, b, (((0,), (0,)), ((), ())), preferred_element_type=F32)


def _ffn_kernel(*refs, prompt, tm, tiles_per_seq, final_norm, nf):
    if prompt:
        (x_ref, nw_ref, wg_ref, wu_ref, cw_ref, cb_ref, wd_ref, fw_ref,
         o_ref, st_ref, xn_ref, hs_ref, p_ref, carry_ref) = refs
    else:
        (x_ref, nw_ref, wg_ref, wu_ref, cw_ref, cb_ref, wd_ref, fw_ref, pg_ref, pu_ref,
         o_ref, nsg_ref, nsu_ref, xn_ref, hs_ref, p_ref) = refs
        prev_refs = (pg_ref, pu_ref)
        new_refs = (nsg_ref, nsu_ref)
    m = pl.program_id(0)
    f = pl.program_id(1)
    S = V7X_SUBLANES
    tf = wg_ref.shape[1]

    def up_into(dst):
        xn = xn_ref[...]
        for half, w_ref in ((0, wg_ref), (1, wu_ref)):
            h = _dot(xn, w_ref[...])
            dst[half, S:, :] = h
            if not prompt:
                new_refs[half][:, 0, :] = prev_refs[half][:, 1, :]
                new_refs[half][:, 1, :] = h

    def conv_gate(src, fp):
        if prompt:
            for half in (0, 1):
                seq_start = (m % tiles_per_seq) == 0
                src[half, 0:S, :] = jnp.where(seq_start, 0.0, carry_ref[fp, half])
                carry_ref[fp, half] = src[half, tm:tm + S, :]
                cols = pl.ds(pl.multiple_of(fp * tf, tf), tf)
                for r in range(2):
                    st_ref[0, r, half:half + 1, cols] = src[half, S + tm - 2 + r:S + tm - 1 + r, :]
        rc = min(64, tm)
        cc = min(256, tf)
        for r0 in range(0, tm, rc):
            for c0 in range(0, tf, cc):
                cs = slice(c0, c0 + cc)
                ys = []
                for half in (0, 1):
                    h = src[half, S + r0:S + r0 + rc, cs]
                    if prompt:
                        h1 = src[half, S - 1 + r0:S - 1 + r0 + rc, cs]
                        h2 = src[half, S - 2 + r0:S - 2 + r0 + rc, cs]
                    else:
                        h2 = prev_refs[half][r0:r0 + rc, 0, cs]
                        h1 = prev_refs[half][r0:r0 + rc, 1, cs]
                    ys.append(cw_ref[2, half:half + 1, cs] * h + cw_ref[1, half:half + 1, cs] * h1
                              + cw_ref[0, half:half + 1, cs] * h2 + cb_ref[half:half + 1, cs])
                p_ref[r0:r0 + rc, cs] = (_silu(ys[0]) * ys[1]).astype(BF16)

    @pl.when(f == 0)
    def _():
        x = x_ref[...]
        xn_ref[...] = _rms_bf16(x, nw_ref[...])
        o_ref[...] = x

    up_into(hs_ref)
    conv_gate(hs_ref, f)
    o_ref[...] += _dot(p_ref[...], wd_ref[...])

    if final_norm:
        @pl.when(f == nf - 1)
        def _():
            o = o_ref[...]
            ms = jnp.mean(o * o, axis=-1, keepdims=True)
            o_ref[...] = o * lax.rsqrt(ms + EPS) * fw_ref[...]


def _ffn(x2d, seq_len, layer, nw, w_up, w_dw, b_dw, w_down, fw, prev=None, final_norm=False):
    T, D = x2d.shape
    F = w_down.shape[1]
    prompt = prev is None
    tm = min(FFN_TM, seq_len) if prompt else T
    tf = FFN_TF if F % FFN_TF == 0 else F
    nf = F // tf
    nm = T // tm
    cw = w_dw.reshape(3, 2, F)
    cb = b_dw.reshape(2, F)
    in_specs = [
        pl.BlockSpec((tm, D), lambda m, f: (m, 0)),
        pl.BlockSpec((1, D), lambda m, f: (0, 0)),
        pl.BlockSpec((None, D, tf), lambda m, f: (layer, 0, f)),
        pl.BlockSpec((None, D, tf), lambda m, f: (layer, 0, nf + f)),
        pl.BlockSpec((3, 2, tf), lambda m, f: (0, 0, f)),
        pl.BlockSpec((2, tf), lambda m, f: (0, f)),
        pl.BlockSpec((None, tf, D), lambda m, f: (layer, f, 0)),
        pl.BlockSpec((1, D), lambda m, f: (0, 0)),
    ]
    args = [x2d, nw.reshape(1, D), w_up, w_up, cw, cb, w_down, fw.reshape(1, D)]
    scratch = [pltpu.VMEM((tm, D), BF16), pltpu.VMEM((2, tm + V7X_SUBLANES, tf), F32),
               pltpu.VMEM((tm, tf), BF16)]
    if prompt:
        tiles_per_seq = seq_len // tm
        nb = T // seq_len
        out_shape = (jax.ShapeDtypeStruct((T, D), F32),
                     jax.ShapeDtypeStruct((nb, 2, 2, F), F32))
        out_specs = (pl.BlockSpec((tm, D), lambda m, f: (m, 0)),
                     pl.BlockSpec((1, 2, 2, F), lambda m, f: (m // tiles_per_seq, 0, 0, 0)))
        scratch += [pltpu.VMEM((nf, 2, V7X_SUBLANES, tf), F32)]
    else:
        tiles_per_seq = 1
        in_specs += [pl.BlockSpec((None, tm, 2, tf), lambda m, f: (layer, m, 0, f)),
                     pl.BlockSpec((None, tm, 2, tf), lambda m, f: (layer, m, 0, nf + f))]
        args += [prev, prev]
        half_state = jax.ShapeDtypeStruct((T, 2, F), F32)
        half_spec = pl.BlockSpec((tm, 2, tf), lambda m, f: (m, 0, f))
        out_shape = (jax.ShapeDtypeStruct((T, D), F32), half_state, half_state)
        out_specs = (pl.BlockSpec((tm, D), lambda m, f: (m, 0)), half_spec, half_spec)
    kern = functools.partial(_ffn_kernel, prompt=prompt, tm=tm, tiles_per_seq=tiles_per_seq,
                             final_norm=final_norm, nf=nf)
    return pl.pallas_call(
        kern, out_shape=out_shape, grid=(nm, nf), in_specs=in_specs, out_specs=out_specs,
        scratch_shapes=scratch, compiler_params=_params(2),
        name="ffn_prompt" if prompt else "ffn_sample")(*args)


def _ln_silu_bf16(c, g, b):
    mu = jnp.mean(c, axis=-1, keepdims=True)
    d = c - mu
    var = jnp.mean(d * d, axis=-1, keepdims=True)
    return _silu(d * lax.rsqrt(var + EPS) * g + b).astype(BF16)


def _conv_prompt_kernel(x_ref, nw_ref, w1_ref, b1_ref, cw_ref, cb_ref, lg_ref, lb_ref,
                        w2_ref, b2_ref, o_ref, st_ref, us_ref, c_ref, cwb_ref,
                        *, tl, nt, dc, width):
    t = pl.program_id(1)
    S = V7X_SUBLANES
    halo = 32
    lead = halo - (width - 1)
    x = x_ref[...]
    xn = _rms_bf16(x, nw_ref[...])
    h = _dot(xn, w1_ref[...]) + b1_ref[...]
    u = h[:, :dc] * jax.nn.sigmoid(h[:, dc:])

    @pl.when(t == 0)
    def _():
        us_ref[0:halo, :] = jnp.zeros((halo, dc), F32)
        for j in range(width):
            cwb_ref[j] = jnp.broadcast_to(cw_ref[j:j + 1, :], (S, dc))

    us_ref[halo:, :] = u

    rb = min(128, tl)
    cwid = V7X_LANES

    def col_body(ci, carry):
        c0 = pl.multiple_of(ci * cwid, cwid)
        cols = pl.ds(c0, cwid)
        for r0 in range(0, tl, rb):
            acc = jnp.broadcast_to(cb_ref[:, cols], (rb // S, S, cwid))
            for a in range(S):
                offs = [lead + j for j in range(width) if (lead + j) % S == a]
                if not offs:
                    continue
                start = offs[0] - a
                span = -(-(rb + offs[-1] - start) // S) * S
                blk = us_ref[pl.ds(r0 + start, span), cols]
                if a:
                    blk = pltpu.roll(blk, span - a, axis=0)
                for off in offs:
                    d = off - offs[0]
                    acc = acc + cwb_ref[off - lead, :, cols] * blk[d:d + rb].reshape(rb // S, S, cwid)
            c_ref[pl.ds(r0, rb), cols] = acc.reshape(rb, cwid)
        return carry

    lax.fori_loop(0, dc // cwid, col_body, 0)

    @pl.when(t == nt - 1)
    def _():
        st_ref[0] = us_ref[halo + tl - (width - 1):halo + tl, :]

    us_ref[0:halo, :] = u[tl - halo:, :]

    c = _ln_silu_bf16(c_ref[...], lg_ref[...], lb_ref[...])
    o_ref[...] = x + _dot(c, w2_ref[...]) + b2_ref[...]


def _conv_prompt(x2d, nb, seq_len, layer, nw, w1, b1, cw, cb, lg, lb, w2, b2):
    T, D = x2d.shape
    dc = w2.shape[1]
    width = cw.shape[0]
    tl = min(CONV_TL, seq_len)
    nt = seq_len // tl
    row = lambda v: v.reshape(1, -1)
    kern = functools.partial(_conv_prompt_kernel, tl=tl, nt=nt, dc=dc, width=width)
    const2 = lambda shape: pl.BlockSpec(shape, lambda b, t: (0, 0))
    return pl.pallas_call(
        kern,
        out_shape=(jax.ShapeDtypeStruct((T, D), F32),
                   jax.ShapeDtypeStruct((nb, width - 1, dc), F32)),
        grid=(nb, nt),
        in_specs=[
            pl.BlockSpec((tl, D), lambda b, t: (b * nt + t, 0)),
            const2((1, D)),
            _resident_layer((D, 2 * dc), layer),
            const2((1, 2 * dc)),
            const2((width, dc)),
            const2((1, dc)), const2((1, dc)), const2((1, dc)),
            _resident_layer((dc, D), layer),
            const2((1, D)),
        ],
        out_specs=(pl.BlockSpec((tl, D), lambda b, t: (b * nt + t, 0)),
                   pl.BlockSpec((1, width - 1, dc), lambda b, t: (b, 0, 0))),
        scratch_shapes=[pltpu.VMEM((32 + tl, dc), F32), pltpu.VMEM((tl, dc), F32),
                        pltpu.VMEM((width, V7X_SUBLANES, dc), F32)],
        compiler_params=_params(2), name="conv_prompt",
    )(x2d, row(nw), w1, row(b1), cw, row(cb), row(lg), row(lb), w2, row(b2))


def _conv_sample_a_kernel(*refs, width, n_earlier):
    x_ref, nw_ref, wa_ref, wg_ref, ba_ref, bg_ref, cw_ref, cb_ref, st_ref = refs[:9]
    c_ref, ns_all_ref, xn_ref = refs[9 + n_earlier:]
    if n_earlier:
        for l, e_ref in enumerate(refs[9:9 + n_earlier]):
            ns_all_ref[l] = e_ref[...]
        ns_ref = ns_all_ref.at[n_earlier]
    else:
        ns_ref = ns_all_ref
    n = pl.program_id(0)

    @pl.when(n == 0)
    def _():
        xn_ref[...] = _rms_bf16(x_ref[...], nw_ref[...])

    xn = xn_ref[...]
    u = (_dot(xn, wa_ref[...]) + ba_ref[...]) * jax.nn.sigmoid(_dot(xn, wg_ref[...]) + bg_ref[...])
    acc = cb_ref[...] + cw_ref[width - 1:width, :] * u
    for j in range(width - 1):
        row = st_ref[j]
        acc = acc + cw_ref[j:j + 1, :] * row
        if j > 0:
            ns_ref[j - 1] = row
    ns_ref[width - 2] = u
    c_ref[...] = acc


def _conv_sample_b_kernel(c_ref, x_ref, lg_ref, lb_ref, w2_ref, b2_ref, o_ref, cn_ref):
    n = pl.program_id(0)

    @pl.when(n == 0)
    def _():
        cn_ref[...] = _ln_silu_bf16(c_ref[...], lg_ref[...], lb_ref[...])

    o_ref[...] = x_ref[...] + _dot(cn_ref[...], w2_ref[...]) + b2_ref[...]


def _conv_sample(x2d, state, layer, nw, w1, b1, cw, cb, lg, lb, w2, b2, earlier=()):
    NB, D = x2d.shape
    dc = w2.shape[1]
    width = cw.shape[0]
    tn = min(256, dc)
    nn = dc // tn
    n_earlier = len(earlier)
    row = lambda v: v.reshape(1, -1)
    one_state = pl.BlockSpec((width - 1, NB, tn), lambda n: (0, 0, n))
    if n_earlier:
        st_shape = (n_earlier + 1, width - 1, NB, dc)
        st_spec = pl.BlockSpec((n_earlier + 1, width - 1, NB, tn), lambda n: (0, 0, 0, n))
    else:
        st_shape, st_spec = (width - 1, NB, dc), one_state
    c, new_state = pl.pallas_call(
        functools.partial(_conv_sample_a_kernel, width=width, n_earlier=n_earlier),
        out_shape=(jax.ShapeDtypeStruct((NB, dc), F32), jax.ShapeDtypeStruct(st_shape, F32)),
        grid=(nn,),
        in_specs=[
            pl.BlockSpec((NB, D), lambda n: (0, 0)),
            pl.BlockSpec((1, D), lambda n: (0, 0)),
            pl.BlockSpec((None, D, tn), lambda n: (layer, 0, n)),
            pl.BlockSpec((None, D, tn), lambda n: (layer, 0, nn + n)),
            pl.BlockSpec((1, tn), lambda n: (0, n)),
            pl.BlockSpec((1, tn), lambda n: (0, nn + n)),
            pl.BlockSpec((width, tn), lambda n: (0, n)),
            pl.BlockSpec((1, tn), lambda n: (0, n)),
            pl.BlockSpec((None, width - 1, NB, tn), lambda n: (layer, 0, 0, n)),
        ] + [one_state] * n_earlier,
        out_specs=(pl.BlockSpec((NB, tn), lambda n: (0, n)), st_spec),
        scratch_shapes=[pltpu.VMEM((NB, D), BF16)],
        compiler_params=_params(1), name="conv_sample_a",
    )(x2d, row(nw), w1, w1, row(b1), row(b1), cw, row(cb), state, *earlier)
    tn2 = min(512, D)
    y = pl.pallas_call(
        _conv_sample_b_kernel,
        out_shape=jax.ShapeDtypeStruct((NB, D), F32),
        grid=(D // tn2,),
        in_specs=[
            pl.BlockSpec((NB, dc), lambda n: (0, 0)),
            pl.BlockSpec((NB, tn2), lambda n: (0, n)),
            pl.BlockSpec((1, dc), lambda n: (0, 0)),
            pl.BlockSpec((1, dc), lambda n: (0, 0)),
            pl.BlockSpec((None, dc, tn2), lambda n: (layer, 0, n)),
            pl.BlockSpec((1, tn2), lambda n: (0, n)),
        ],
        out_specs=pl.BlockSpec((NB, tn2), lambda n: (0, n)),
        scratch_shapes=[pltpu.VMEM((NB, dc), BF16)],
        compiler_params=_params(1), name="conv_sample_b",
    )(c, x2d, row(lg), row(lb), w2, row(b2))
    return y, new_state


def _proj_kernel(x_ref, nw_ref, w_ref, o_ref, xn_ref, *, acts):
    s = pl.program_id(1)

    @pl.when(s == 0)
    def _():
        xn_ref[...] = _rms_bf16(x_ref[...], nw_ref[...])

    silu_segs = [i for i, a in enumerate(acts) if a]
    is_silu = functools.reduce(jnp.logical_or, [s == i for i in silu_segs])
    n = w_ref.shape[2]
    cw = 512 if n % 512 == 0 else n

    def segment(act):
        xn = xn_ref[...]
        for c0 in range(0, n, cw):
            y = _dot(xn, w_ref[0, :, c0:c0 + cw])
            o_ref[0, :, c0:c0 + cw] = _silu(y) if act else y

    @pl.when(is_silu)
    def _():
        segment(True)

    @pl.when(jnp.logical_not(is_silu))
    def _():
        segment(False)


def _proj(x2d, nw, w_stack, acts, tm):
    T, D = x2d.shape
    S, _, N = w_stack.shape
    return pl.pallas_call(
        functools.partial(_proj_kernel, acts=acts),
        out_shape=jax.ShapeDtypeStruct((S, T, N), F32),
        grid=(T // tm, S),
        in_specs=[pl.BlockSpec((tm, D), lambda m, s: (m, 0)),
                  pl.BlockSpec((1, D), lambda m, s: (0, 0)),
                  pl.BlockSpec((1, D, N), lambda m, s: (s, 0, 0))],
        out_specs=pl.BlockSpec((1, tm, N), lambda m, s: (s, m, 0)),
        scratch_shapes=[pltpu.VMEM((tm, D), BF16)],
        compiler_params=_params(2), name="hgrn_proj",
    )(x2d, nw.reshape(1, D), w_stack)


def _residual_matmul_kernel(a_ref, w_ref, x_ref, o_ref):
    o_ref[...] = x_ref[...] + _dot(a_ref[...], w_ref[...])


def _residual_matmul(a, w, layer, x2d, tm):
    T, K = a.shape
    N = w.shape[2]
    return pl.pallas_call(
        _residual_matmul_kernel,
        out_shape=jax.ShapeDtypeStruct((T, N), F32),
        grid=(T // tm,),
        in_specs=[pl.BlockSpec((tm, K), lambda m: (m, 0)),
                  _resident_layer((K, N), layer),
                  pl.BlockSpec((tm, N), lambda m: (m, 0))],
        out_specs=pl.BlockSpec((tm, N), lambda m: (m, 0)),
        compiler_params=_params(1), name="hgrn_out",
    )(a, w, x2d)


def _forget_lower_bound(raw, layer):
    mx = jnp.max(raw, axis=0, keepdims=True)
    e = jnp.exp(raw - mx)
    sm = e / jnp.sum(e, axis=0, keepdims=True)
    first = sm[0:1]
    cum = first
    for i in range(1, layer + 1):
        cum = cum + sm[i:i + 1]
    return cum - first


def _log_forget_and_key(z, lb):
    t = jnp.exp(-jnp.abs(z))
    r = 1.0 / (1.0 + t)
    sig_neg = jnp.where(z >= 0.0, t * r, r)
    log_sig = jnp.minimum(z, 0.0) - jnp.log(1.0 + t)
    a = jnp.log(lb)
    b = jnp.log1p(-lb) + log_sig
    log_f = jnp.maximum(a, b) + jnp.log(1.0 + jnp.exp(-jnp.abs(a - b)))
    k = (1.0 - lb) * sig_neg
    return log_f, k


def _split3_bf16(x):
    hi = x.astype(BF16)
    r1 = x - hi.astype(F32)
    mid = r1.astype(BF16)
    lo = (r1 - mid.astype(F32)).astype(BF16)
    return hi, mid, lo


def _head_norm_gate(o, ng, gate):
    ms = jnp.mean(o * o, axis=-1, keepdims=True)
    return (o * lax.rsqrt(ms + EPS) * ng * gate).astype(BF16)


def _gla_prompt_kernel(q_ref, z_ref, v_ref, g_ref, lbr_ref, ng_ref, o_ref, s_ref,
                       st_ref, gh_ref, oacc_ref, x_ref, w_ref, ad_ref, *, layer, tl, nt, hg):
    C = min(GLA_CHUNK, tl)
    sub = min(GLA_SUB, C)
    n_sub = C // sub
    W = V7X_LANES
    t_idx = pl.program_id(2)

    @pl.when(t_idx == 0)
    def _():
        st_ref[...] = jnp.zeros(st_ref.shape, F32)
        r = lax.broadcasted_iota(jnp.int32, (sub * W, W), 0) // W
        c = lax.broadcasted_iota(jnp.int32, (sub * W, W), 1)
        for i in range(n_sub):
            w_ref[i] = (c == r + i * sub).astype(BF16)

    lb_all = _forget_lower_bound(lbr_ref[...], layer)
    tri = (lax.broadcasted_iota(jnp.int32, (C, C), 0)
           >= lax.broadcasted_iota(jnp.int32, (C, C), 1)).astype(BF16)
    a_row = lax.broadcasted_iota(jnp.int32, (sub, C), 0)
    a_col = lax.broadcasted_iota(jnp.int32, (sub, C), 1)
    zero_rows = jnp.zeros((V7X_SUBLANES, W), F32)

    n_chunks = tl // C

    def chunk_rows(ci):
        start = ci * C if isinstance(ci, int) else pl.multiple_of(ci * C, C)
        return pl.ds(start, C)

    def decay_terms(ci, slot):
        rows = chunk_rows(ci)
        log_f, k = _log_forget_and_key(z_ref[0, 0, rows, :], lb_all)
        hi, mid, lo = _split3_bf16(log_f)
        G = (_dot(tri, hi) + _dot(tri, mid) + _dot(tri, lo)) * LOG2E
        gh_ref[slot, 0] = G
        gh_ref[slot, 1] = G - jnp.log2(k)
        gh_ref[slot, 2] = k

    def dense(ci, slot):
        rows = chunk_rows(ci)
        for hh in range(hg):
            ls = slice(hh * W, (hh + 1) * W)
            q = q_ref[0, 0, rows, ls]
            v = v_ref[0, 0, rows, ls].astype(BF16)
            G = gh_ref[slot, 0, :, ls]
            k = gh_ref[slot, 2, :, ls]
            g_last = G[C - 1:C, :]
            st = st_ref[hh]
            o = _dot_nt((q * jnp.exp2(G)).astype(BF16), st.astype(BF16))
            off_blocks = [jnp.zeros((sub, C), BF16)]
            for i in range(n_sub):
                base = i * sub
                Gi, qi = G[base:base + sub], q[base:base + sub]
                for s in range(sub):
                    lo_row = (s // V7X_SUBLANES) * V7X_SUBLANES
                    h_s = gh_ref[slot, 1, base + s:base + s + 1, ls]
                    e = jnp.exp2(jnp.minimum(Gi[lo_row:] - h_s, 0.0)) * qi[lo_row:]
                    if lo_row:
                        e = jnp.concatenate([zero_rows] * (lo_row // V7X_SUBLANES) + [e], axis=0)
                    x_ref[i, hh * sub:(hh + 1) * sub, s * W:(s + 1) * W] = e.astype(BF16)
                if i > 0:
                    g_prev = gh_ref[slot, 0, base - 1:base, ls]
                    qt = (qi * jnp.exp2(Gi - g_prev)).astype(BF16)
                    kt = jnp.concatenate(
                        [(k[:base] * jnp.exp2(g_prev - G[:base])).astype(BF16),
                         jnp.zeros((C - base, W), BF16)], axis=0)
                    off_blocks.append(_dot_nt(qt, kt).astype(BF16))
            a_off = jnp.concatenate(off_blocks, axis=0) if n_sub > 1 else off_blocks[0]
            oacc_ref[hh] = o + _dot(a_off, v)
            kd = (k * jnp.exp2(g_last - G)).astype(BF16)
            st_ref[hh] = jnp.exp2(g_last) * st + _dot_tn(v, kd)

    def finish(ci):
        rows = chunk_rows(ci)
        for i in range(n_sub):
            ad_ref[i] = _dot(x_ref[i], w_ref[i])
        for hh in range(hg):
            ls = slice(hh * W, (hh + 1) * W)
            blocks = [jnp.where(a_col <= a_row + i * sub,
                                ad_ref[i, hh * sub:(hh + 1) * sub, 0:C], 0.0).astype(BF16)
                      for i in range(n_sub)]
            a_diag = jnp.concatenate(blocks, axis=0) if n_sub > 1 else blocks[0]
            o = oacc_ref[hh] + _dot(a_diag, v_ref[0, 0, rows, ls].astype(BF16))
            o_ref[0, rows, ls] = _head_norm_gate(o, ng_ref[:, ls], g_ref[0, 0, rows, ls])

    decay_terms(0, 0)

    def chunk(ci, carry):
        slot = ci % 2
        dense(ci, slot)
        finish(ci)
        decay_terms(jnp.minimum(ci + 1, n_chunks - 1), 1 - slot)
        return carry

    lax.fori_loop(0, n_chunks, chunk, 0)

    @pl.when(t_idx == nt - 1)
    def _():
        for hh in range(hg):
            s_ref[0, hh] = st_ref[hh].T


def _gla_prompt(P, lb_raw, layer, ng, n_heads):
    _, B, L, D = P.shape
    W = D // n_heads
    assert W == V7X_LANES
    hg = min(GLA_HEADS_PER_STEP, n_heads)
    tl = min(GLA_TL, L)
    nt = L // tl
    C = min(GLA_CHUNK, tl)
    sub = min(GLA_SUB, C)
    n_sub = C // sub
    NL = lb_raw.shape[0]
    seg = lambda s: pl.BlockSpec((1, 1, tl, hg * W), lambda b, h, t: (s, b, t, h))
    return pl.pallas_call(
        functools.partial(_gla_prompt_kernel, layer=layer, tl=tl, nt=nt, hg=hg),
        out_shape=(jax.ShapeDtypeStruct((B, L, D), BF16),
                   jax.ShapeDtypeStruct((B, n_heads, W, W), F32)),
        grid=(B, n_heads // hg, nt),
        in_specs=[seg(0), seg(1), seg(2), seg(3),
                  pl.BlockSpec((NL, hg * W), lambda b, h, t: (0, h)),
                  pl.BlockSpec((1, hg * W), lambda b, h, t: (0, h))],
        out_specs=(pl.BlockSpec((1, tl, hg * W), lambda b, h, t: (b, t, h)),
                   pl.BlockSpec((1, hg, W, W), lambda b, h, t: (b, h, 0, 0))),
        scratch_shapes=[pltpu.VMEM((hg, W, W), F32),
                        pltpu.VMEM((2, 3, C, hg * W), F32),
                        pltpu.VMEM((hg, C, W), F32),
                        pltpu.VMEM((n_sub, hg * sub, sub * W), BF16),
                        pltpu.VMEM((n_sub, sub * W, W), BF16),
                        pltpu.VMEM((n_sub, hg * sub, W), F32)],
        compiler_params=_params(3), name="gla_prompt",
    )(P, P, P, P, lb_raw, ng.reshape(1, D))


def _gla_sample_kernel(*refs, layer, bb, n_earlier):
    q_ref, z_ref, v_ref, g_ref, lbr_ref, ng_ref, s_ref = refs[:7]
    earlier = refs[7:7 + n_earlier]
    o_ref, ns_all_ref, ft_ref, kt_ref, vb_ref = refs[7 + n_earlier:]
    if n_earlier:
        for l, e_ref in enumerate(earlier):
            ns_all_ref[l] = e_ref[...]
        ns_ref = ns_all_ref.at[n_earlier]
    else:
        ns_ref = ns_all_ref
    j = pl.program_id(1)

    @pl.when(j == 0)
    def _():
        lb = _forget_lower_bound(lbr_ref[...], layer)
        log_f, k = _log_forget_and_key(z_ref[0], lb)
        ft_ref[...] = jnp.exp(log_f).T
        kt_ref[...] = k.T
        vb_ref[...] = v_ref[0].astype(BF16)

    shift = (V7X_LANES - j * bb) % V7X_LANES
    ft = pltpu.roll(ft_ref[...], shift, axis=1)
    kt = kt_ref[...]
    vb = vb_ref[...]
    r0 = pl.multiple_of(j * bb, bb)
    qb = q_ref[0, pl.ds(r0, bb), :].astype(BF16)
    lane_b = lax.broadcasted_iota(jnp.int32, kt.shape, 1)
    row_b = lax.broadcasted_iota(jnp.int32, (bb, vb.shape[1]), 0)
    o_acc = jnp.zeros((bb, vb.shape[1]), F32)
    for i in range(bb):
        k_b = jnp.where(lane_b == r0 + i, kt, 0.0).astype(BF16)
        s_new = ft[:, i:i + 1] * s_ref[i, 0] + _dot(k_b, vb)
        ns_ref[i, 0] = s_new
        o_acc = jnp.where(row_b == i, _dot(qb, s_new.astype(BF16)), o_acc)
    o_ref[...] = _head_norm_gate(o_acc, ng_ref[...], g_ref[0, pl.ds(r0, bb), :])


def _gla_sample(P, state, lb_raw, layer, ng, earlier=()):
    _, NB, D = P.shape
    _, _, H, DK, DV = state.shape
    assert NB == V7X_LANES and DK == V7X_LANES and DV == V7X_LANES
    bb = 32
    NL = lb_raw.shape[0]
    n_earlier = len(earlier)
    seg = lambda s: pl.BlockSpec((1, NB, DK), lambda h, j: (s, 0, h))
    one_state = pl.BlockSpec((bb, 1, DK, DV), lambda h, j: (j, h, 0, 0))
    if n_earlier:
        st_shape = (n_earlier + 1,) + state.shape[1:]
        st_spec = pl.BlockSpec((n_earlier + 1, bb, 1, DK, DV), lambda h, j: (0, j, h, 0, 0))
    else:
        st_shape, st_spec = state.shape[1:], one_state
    return pl.pallas_call(
        functools.partial(_gla_sample_kernel, layer=layer, bb=bb, n_earlier=n_earlier),
        out_shape=(jax.ShapeDtypeStruct((NB, D), BF16), jax.ShapeDtypeStruct(st_shape, F32)),
        grid=(H, NB // bb),
        in_specs=[seg(0), seg(1), seg(2), seg(3),
                  pl.BlockSpec((NL, DK), lambda h, j: (0, h)),
                  pl.BlockSpec((1, DV), lambda h, j: (0, h)),
                  pl.BlockSpec((None, bb, 1, DK, DV), lambda h, j: (layer, j, h, 0, 0))]
                 + [one_state] * n_earlier,
        out_specs=(pl.BlockSpec((bb, DV), lambda h, j: (j, h)), st_spec),
        scratch_shapes=[pltpu.VMEM((DK, NB), F32), pltpu.VMEM((DK, NB), F32),
                        pltpu.VMEM((NB, DV), BF16)],
        compiler_params=_params(2), name="gla_sample",
    )(P, P, P, P, lb_raw, ng.reshape(1, D), state, *earlier)


def kernel(x_prompt, x_sample, state_conv, state_hgrn, state_ffn, norm_mix, norm_ffn, norm_final,
           conv_w_pw1, conv_b_pw1, conv_w_dw, conv_b_dw, conv_ln_g, conv_ln_b, conv_w_pw2,
           conv_b_pw2, hgrn_w_q, hgrn_w_f, hgrn_w_i, hgrn_w_g, hgrn_w_o, hgrn_lb_raw,
           hgrn_norm_g, ffn_w_up, ffn_w_dw, ffn_b_dw, ffn_w_down):
    B, L, D = x_prompt.shape
    NB = x_sample.shape[0]
    depth = ffn_w_up.shape[0]
    n_heads = state_hgrn.shape[2]
    F = ffn_w_down.shape[1]
    lb_raw = hgrn_lb_raw.astype(F32)

    xp = x_prompt.reshape(B * L, D)
    xs = x_sample.reshape(NB, D)
    conv_p, hgrn_p, ffn_p, conv_s, hgrn_s, ffn_s = [], [], [], [], [], []
    hgrn_s_all = conv_s_all = None
    acts = (True, False, False, True)
    w_pw1 = conv_w_pw1.astype(BF16)
    w_pw2 = conv_w_pw2.astype(BF16)
    w_o = hgrn_w_o.astype(BF16)
    w_up = ffn_w_up.astype(BF16)
    w_down = ffn_w_down.astype(BF16)
    state_conv_t = state_conv.transpose(0, 2, 1, 3)
    for i in range(depth):
        j = i // 2
        if i % 2 == 0:
            cargs = (j, norm_mix[i], w_pw1, conv_b_pw1[j], conv_w_dw[j], conv_b_dw[j],
                     conv_ln_g[j], conv_ln_b[j], w_pw2, conv_b_pw2[j])
            xp, st = _conv_prompt(xp, B, L, *cargs)
            conv_p.append(st)
            if j == state_conv.shape[0] - 1 and conv_s:
                xs, conv_s_all = _conv_sample(xs, state_conv_t, *cargs, earlier=tuple(conv_s))
            else:
                xs, st = _conv_sample(xs, state_conv_t, *cargs)
                conv_s.append(st)
        else:
            w_stack = jnp.stack([hgrn_w_q[j], hgrn_w_f[j], hgrn_w_i[j], hgrn_w_g[j]]).astype(BF16)
            P = _proj(xp, norm_mix[i], w_stack, acts, tm=min(PROJ_TM, L))
            og, st = _gla_prompt(P.reshape(4, B, L, D), lb_raw, j, hgrn_norm_g[j], n_heads)
            hgrn_p.append(st)
            xp = _residual_matmul(og.reshape(B * L, D), w_o, j, xp, tm=min(PROJ_TM, L))
            P = _proj(xs, norm_mix[i], w_stack, acts, tm=NB)
            if j == state_hgrn.shape[0] - 1 and hgrn_s:
                og, hgrn_s_all = _gla_sample(P, state_hgrn, lb_raw, j, hgrn_norm_g[j],
                                             earlier=tuple(hgrn_s))
            else:
                og, st = _gla_sample(P, state_hgrn, lb_raw, j, hgrn_norm_g[j])
                hgrn_s.append(st)
            xs = _residual_matmul(og, w_o, j, xs, tm=NB)
        last = i == depth - 1
        fargs = (i, norm_ffn[i], w_up, ffn_w_dw[i], ffn_b_dw[i], w_down, norm_final)
        xp, st = _ffn(xp, L, *fargs, final_norm=last)
        ffn_p.append(st.reshape(B, 2, 2 * F))
        xs, st_gate, st_up = _ffn(xs, 1, *fargs, prev=state_ffn, final_norm=last)
        ffn_s.append((st_gate, st_up))
    return (xp.reshape(B, L, D), xs.reshape(NB, 1, D),
            jnp.stack(conv_p), jnp.stack(hgrn_p), jnp.stack(ffn_p),
            (conv_s_all if conv_s_all is not None else jnp.stack(conv_s)).transpose(0, 2, 1, 3),
            hgrn_s_all if hgrn_s_all is not None else jnp.stack(hgrn_s),
            jnp.concatenate([jnp.stack([g for g, _ in ffn_s]), jnp.stack([u for _, u in ffn_s])],
                            axis=-1))
```

```python
import functools

import jax
import jax.numpy as jnp
from jax import lax
from jax.experimental import pallas as pl
from jax.experimental.pallas import tpu as pltpu

F32 = jnp.float32
BF16 = jnp.bfloat16
EPS = 1e-6
LOG2E = 1.4426950408889634

V7X_VMEM_BYTES = 64 * 1024 * 1024
V7X_LANES = 128
V7X_SUBLANES = 8
V7X_MXU_DIM = 256
VMEM_LIMIT = V7X_VMEM_BYTES - 8 * 1024 * 1024

GLA_CHUNK = 64
GLA_SUB = 16
GLA_HEADS_PER_STEP = 8
GLA_TL = 1024
FFN_TM = 512
FFN_TF = 512
CONV_TL = 256
PROJ_TM = 512


def _params(n_axes, vmem=VMEM_LIMIT):
    return pltpu.CompilerParams(
        dimension_semantics=("arbitrary",) * n_axes, vmem_limit_bytes=vmem)


def _resident_layer(shape, layer):
    n = len(shape)
    return pl.BlockSpec((None,) + tuple(shape), lambda *_: (layer,) + (0,) * n,
                        pipeline_mode=pl.Buffered(1))


def _rms_bf16(x, w):
    ms = jnp.mean(x * x, axis=-1, keepdims=True)
    return (x * lax.rsqrt(ms + EPS) * w).astype(BF16)


def _silu(x):
    return x * jax.nn.sigmoid(x)


def _dot(a, b):
    return jnp.dot(a, b, preferred_element_type=F32)


def _dot_nt(a, b):
    return lax.dot_general(a, b, (((1,), (1,)), ((), ())), preferred_element_type=F32)


def _dot_tn(a, b):
    return lax.dot_general(a, b, (((0,), (0,)), ((), ())), preferred_element_type=F32)


def _ffn_kernel(*refs, prompt, tm, tiles_per_seq, final_norm, nf):
    if prompt:
        (x_ref, nw_ref, wg_ref, wu_ref, cw_ref, cb_ref, wd_ref, fw_ref,
         o_ref, st_ref, xn_ref, hs_ref, p_ref, carry_ref) = refs
    else:
        (x_ref, nw_ref, wg_ref, wu_ref, cw_ref, cb_ref, wd_ref, fw_ref, pg_ref, pu_ref,
         o_ref, nsg_ref, nsu_ref, xn_ref, hs_ref, p_ref) = refs
        prev_refs = (pg_ref, pu_ref)
        new_refs = (nsg_ref, nsu_ref)
    m = pl.program_id(0)
    f = pl.program_id(1)
    S = V7X_SUBLANES
    tf = wg_ref.shape[1]

    def up_into(dst):
        xn = xn_ref[...]
        for half, w_ref in ((0, wg_ref), (1, wu_ref)):
            h = _dot(xn, w_ref[...])
            dst[half, S:, :] = h
            if not prompt:
                new_refs[half][:, 0, :] = prev_refs[half][:, 1, :]
                new_refs[half][:, 1, :] = h

    def conv_gate(src, fp):
        if prompt:
            for half in (0, 1):
                seq_start = (m % tiles_per_seq) == 0
                src[half, 0:S, :] = jnp.where(seq_start, 0.0, carry_ref[fp, half])
                carry_ref[fp, half] = src[half, tm:tm + S, :]
                cols = pl.ds(pl.multiple_of(fp * tf, tf), tf)
                for r in range(2):
                    st_ref[0, r, half:half + 1, cols] = src[half, S + tm - 2 + r:S + tm - 1 + r, :]
        rc = min(64, tm)
        cc = min(256, tf)
        for r0 in range(0, tm, rc):
            for c0 in range(0, tf, cc):
                cs = slice(c0, c0 + cc)
                ys = []
                for half in (0, 1):
                    h = src[half, S + r0:S + r0 + rc, cs]
                    if prompt:
                        h1 = src[half, S - 1 + r0:S - 1 + r0 + rc, cs]
                        h2 = src[half, S - 2 + r0:S - 2 + r0 + rc, cs]
                    else:
                        h2 = prev_refs[half][r0:r0 + rc, 0, cs]
                        h1 = prev_refs[half][r0:r0 + rc, 1, cs]
                    ys.append(cw_ref[2, half:half + 1, cs] * h + cw_ref[1, half:half + 1, cs] * h1
                              + cw_ref[0, half:half + 1, cs] * h2 + cb_ref[half:half + 1, cs])
                p_ref[r0:r0 + rc, cs] = (_silu(ys[0]) * ys[1]).astype(BF16)

    @pl.when(f == 0)
    def _():
        x = x_ref[...]
        xn_ref[...] = _rms_bf16(x, nw_ref[...])
        o_ref[...] = x

    up_into(hs_ref)
    conv_gate(hs_ref, f)
    o_ref[...] += _dot(p_ref[...], wd_ref[...])

    if final_norm:
        @pl.when(f == nf - 1)
        def _():
            o = o_ref[...]
            ms = jnp.mean(o * o, axis=-1, keepdims=True)
            o_ref[...] = o * lax.rsqrt(ms + EPS) * fw_ref[...]


def _ffn(x2d, seq_len, layer, nw, w_up, w_dw, b_dw, w_down, fw, prev=None, final_norm=False):
    T, D = x2d.shape
    F = w_down.shape[1]
    prompt = prev is None
    tm = min(FFN_TM, seq_len) if prompt else T
    tf = FFN_TF if F % FFN_TF == 0 else F
    nf = F // tf
    nm = T // tm
    cw = w_dw.reshape(3, 2, F)
    cb = b_dw.reshape(2, F)
    in_specs = [
        pl.BlockSpec((tm, D), lambda m, f: (m, 0)),
        pl.BlockSpec((1, D), lambda m, f: (0, 0)),
        pl.BlockSpec((None, D, tf), lambda m, f: (layer, 0, f)),
        pl.BlockSpec((None, D, tf), lambda m, f: (layer, 0, nf + f)),
        pl.BlockSpec((3, 2, tf), lambda m, f: (0, 0, f)),
        pl.BlockSpec((2, tf), lambda m, f: (0, f)),
        pl.BlockSpec((None, tf, D), lambda m, f: (layer, f, 0)),
        pl.BlockSpec((1, D), lambda m, f: (0, 0)),
    ]
    args = [x2d, nw.reshape(1, D), w_up, w_up, cw, cb, w_down, fw.reshape(1, D)]
    scratch = [pltpu.VMEM((tm, D), BF16), pltpu.VMEM((2, tm + V7X_SUBLANES, tf), F32),
               pltpu.VMEM((tm, tf), BF16)]
    if prompt:
        tiles_per_seq = seq_len // tm
        nb = T // seq_len
        out_shape = (jax.ShapeDtypeStruct((T, D), F32),
                     jax.ShapeDtypeStruct((nb, 2, 2, F), F32))
        out_specs = (pl.BlockSpec((tm, D), lambda m, f: (m, 0)),
                     pl.BlockSpec((1, 2, 2, F), lambda m, f: (m // tiles_per_seq, 0, 0, 0)))
        scratch += [pltpu.VMEM((nf, 2, V7X_SUBLANES, tf), F32)]
    else:
        tiles_per_seq = 1
        in_specs += [pl.BlockSpec((None, tm, 2, tf), lambda m, f: (layer, m, 0, f)),
                     pl.BlockSpec((None, tm, 2, tf), lambda m, f: (layer, m, 0, nf + f))]
        args += [prev, prev]
        half_state = jax.ShapeDtypeStruct((T, 2, F), F32)
        half_spec = pl.BlockSpec((tm, 2, tf), lambda m, f: (m, 0, f))
        out_shape = (jax.ShapeDtypeStruct((T, D), F32), half_state, half_state)
        out_specs = (pl.BlockSpec((tm, D), lambda m, f: (m, 0)), half_spec, half_spec)
    kern = functools.partial(_ffn_kernel, prompt=prompt, tm=tm, tiles_per_seq=tiles_per_seq,
                             final_norm=final_norm, nf=nf)
    return pl.pallas_call(
        kern, out_shape=out_shape, grid=(nm, nf), in_specs=in_specs, out_specs=out_specs,
        scratch_shapes=scratch, compiler_params=_params(2),
        name="ffn_prompt" if prompt else "ffn_sample")(*args)


def _ln_silu_bf16(c, g, b):
    mu = jnp.mean(c, axis=-1, keepdims=True)
    d = c - mu
    var = jnp.mean(d * d, axis=-1, keepdims=True)
    return _silu(d * lax.rsqrt(var + EPS) * g + b).astype(BF16)


def _conv_prompt_kernel(x_ref, nw_ref, w1_ref, b1_ref, cw_ref, cb_ref, lg_ref, lb_ref,
                        w2_ref, b2_ref, o_ref, st_ref, us_ref, c_ref, cwb_ref,
                        *, tl, nt, dc, width):
    t = pl.program_id(1)
    S = V7X_SUBLANES
    halo = 32
    lead = halo - (width - 1)
    x = x_ref[...]
    xn = _rms_bf16(x, nw_ref[...])
    h = _dot(xn, w1_ref[...]) + b1_ref[...]
    u = h[:, :dc] * jax.nn.sigmoid(h[:, dc:])

    @pl.when(t == 0)
    def _():
        us_ref[0:halo, :] = jnp.zeros((halo, dc), F32)
        for j in range(width):
            cwb_ref[j] = jnp.broadcast_to(cw_ref[j:j + 1, :], (S, dc))

    us_ref[halo:, :] = u

    rb = min(128, tl)
    cwid = V7X_LANES

    def col_body(ci, carry):
        c0 = pl.multiple_of(ci * cwid, cwid)
        cols = pl.ds(c0, cwid)
        for r0 in range(0, tl, rb):
            acc = jnp.broadcast_to(cb_ref[:, cols], (rb // S, S, cwid))
            for a in range(S):
                offs = [lead + j for j in range(width) if (lead + j) % S == a]
                if not offs:
                    continue
                start = offs[0] - a
                span = -(-(rb + offs[-1] - start) // S) * S
                blk = us_ref[pl.ds(r0 + start, span), cols]
                if a:
                    blk = pltpu.roll(blk, span - a, axis=0)
                for off in offs:
                    d = off - offs[0]
                    acc = acc + cwb_ref[off - lead, :, cols] * blk[d:d + rb].reshape(rb // S, S, cwid)
            c_ref[pl.ds(r0, rb), cols] = acc.reshape(rb, cwid)
        return carry

    lax.fori_loop(0, dc // cwid, col_body, 0)

    @pl.when(t == nt - 1)
    def _():
        st_ref[0] = us_ref[halo + tl - (width - 1):halo + tl, :]

    us_ref[0:halo, :] = u[tl - halo:, :]

    c = _ln_silu_bf16(c_ref[...], lg_ref[...], lb_ref[...])
    o_ref[...] = x + _dot(c, w2_ref[...]) + b2_ref[...]


def _conv_prompt(x2d, nb, seq_len, layer, nw, w1, b1, cw, cb, lg, lb, w2, b2):
    T, D = x2d.shape
    dc = w2.shape[1]
    width = cw.shape[0]
    tl = min(CONV_TL, seq_len)
    nt = seq_len // tl
    row = lambda v: v.reshape(1, -1)
    kern = functools.partial(_conv_prompt_kernel, tl=tl, nt=nt, dc=dc, width=width)
    const2 = lambda shape: pl.BlockSpec(shape, lambda b, t: (0, 0))
    return pl.pallas_call(
        kern,
        out_shape=(jax.ShapeDtypeStruct((T, D), F32),
                   jax.ShapeDtypeStruct((nb, width - 1, dc), F32)),
        grid=(nb, nt),
        in_specs=[
            pl.BlockSpec((tl, D), lambda b, t: (b * nt + t, 0)),
            const2((1, D)),
            _resident_layer((D, 2 * dc), layer),
            const2((1, 2 * dc)),
            const2((width, dc)),
            const2((1, dc)), const2((1, dc)), const2((1, dc)),
            _resident_layer((dc, D), layer),
            const2((1, D)),
        ],
        out_specs=(pl.BlockSpec((tl, D), lambda b, t: (b * nt + t, 0)),
                   pl.BlockSpec((1, width - 1, dc), lambda b, t: (b, 0, 0))),
        scratch_shapes=[pltpu.VMEM((32 + tl, dc), F32), pltpu.VMEM((tl, dc), F32),
                        pltpu.VMEM((width, V7X_SUBLANES, dc), F32)],
        compiler_params=_params(2), name="conv_prompt",
    )(x2d, row(nw), w1, row(b1), cw, row(cb), row(lg), row(lb), w2, row(b2))


def _conv_sample_a_kernel(*refs, width, n_earlier):
    x_ref, nw_ref, wa_ref, wg_ref, ba_ref, bg_ref, cw_ref, cb_ref, st_ref = refs[:9]
    c_ref, ns_all_ref, xn_ref = refs[9 + n_earlier:]
    if n_earlier:
        for l, e_ref in enumerate(refs[9:9 + n_earlier]):
            ns_all_ref[l] = e_ref[...]
        ns_ref = ns_all_ref.at[n_earlier]
    else:
        ns_ref = ns_all_ref
    n = pl.program_id(0)

    @pl.when(n == 0)
    def _():
        xn_ref[...] = _rms_bf16(x_ref[...], nw_ref[...])

    xn = xn_ref[...]
    u = (_dot(xn, wa_ref[...]) + ba_ref[...]) * jax.nn.sigmoid(_dot(xn, wg_ref[...]) + bg_ref[...])
    acc = cb_ref[...] + cw_ref[width - 1:width, :] * u
    for j in range(width - 1):
        row = st_ref[j]
        acc = acc + cw_ref[j:j + 1, :] * row
        if j > 0:
            ns_ref[j - 1] = row
    ns_ref[width - 2] = u
    c_ref[...] = acc


def _conv_sample_b_kernel(c_ref, x_ref, lg_ref, lb_ref, w2_ref, b2_ref, o_ref, cn_ref):
    n = pl.program_id(0)

    @pl.when(n == 0)
    def _():
        cn_ref[...] = _ln_silu_bf16(c_ref[...], lg_ref[...], lb_ref[...])

    o_ref[...] = x_ref[...] + _dot(cn_ref[...], w2_ref[...]) + b2_ref[...]


def _conv_sample(x2d, state, layer, nw, w1, b1, cw, cb, lg, lb, w2, b2, earlier=()):
    NB, D = x2d.shape
    dc = w2.shape[1]
    width = cw.shape[0]
    tn = min(256, dc)
    nn = dc // tn
    n_earlier = len(earlier)
    row = lambda v: v.reshape(1, -1)
    one_state = pl.BlockSpec((width - 1, NB, tn), lambda n: (0, 0, n))
    if n_earlier:
        st_shape = (n_earlier + 1, width - 1, NB, dc)
        st_spec = pl.BlockSpec((n_earlier + 1, width - 1, NB, tn), lambda n: (0, 0, 0, n))
    else:
        st_shape, st_spec = (width - 1, NB, dc), one_state
    c, new_state = pl.pallas_call(
        functools.partial(_conv_sample_a_kernel, width=width, n_earlier=n_earlier),
        out_shape=(jax.ShapeDtypeStruct((NB, dc), F32), jax.ShapeDtypeStruct(st_shape, F32)),
        grid=(nn,),
        in_specs=[
            pl.BlockSpec((NB, D), lambda n: (0, 0)),
            pl.BlockSpec((1, D), lambda n: (0, 0)),
            pl.BlockSpec((None, D, tn), lambda n: (layer, 0, n)),
            pl.BlockSpec((None, D, tn), lambda n: (layer, 0, nn + n)),
            pl.BlockSpec((1, tn), lambda n: (0, n)),
            pl.BlockSpec((1, tn), lambda n: (0, nn + n)),
            pl.BlockSpec((width, tn), lambda n: (0, n)),
            pl.BlockSpec((1, tn), lambda n: (0, n)),
            pl.BlockSpec((None, width - 1, NB, tn), lambda n: (layer, 0, 0, n)),
        ] + [one_state] * n_earlier,
        out_specs=(pl.BlockSpec((NB, tn), lambda n: (0, n)), st_spec),
        scratch_shapes=[pltpu.VMEM((NB, D), BF16)],
        compiler_params=_params(1), name="conv_sample_a",
    )(x2d, row(nw), w1, w1, row(b1), row(b1), cw, row(cb), state, *earlier)
    tn2 = min(512, D)
    y = pl.pallas_call(
        _conv_sample_b_kernel,
        out_shape=jax.ShapeDtypeStruct((NB, D), F32),
        grid=(D // tn2,),
        in_specs=[
            pl.BlockSpec((NB, dc), lambda n: (0, 0)),
            pl.BlockSpec((NB, tn2), lambda n: (0, n)),
            pl.BlockSpec((1, dc), lambda n: (0, 0)),
            pl.BlockSpec((1, dc), lambda n: (0, 0)),
            pl.BlockSpec((None, dc, tn2), lambda n: (layer, 0, n)),
            pl.BlockSpec((1, tn2), lambda n: (0, n)),
        ],
        out_specs=pl.BlockSpec((NB, tn2), lambda n: (0, n)),
        scratch_shapes=[pltpu.VMEM((NB, dc), BF16)],
        compiler_params=_params(1), name="conv_sample_b",
    )(c, x2d, row(lg), row(lb), w2, row(b2))
    return y, new_state


def _proj_kernel(x_ref, nw_ref, w_ref, o_ref, xn_ref, *, acts):
    s = pl.program_id(1)

    @pl.when(s == 0)
    def _():
        xn_ref[...] = _rms_bf16(x_ref[...], nw_ref[...])

    silu_segs = [i for i, a in enumerate(acts) if a]
    is_silu = functools.reduce(jnp.logical_or, [s == i for i in silu_segs])
    n = w_ref.shape[2]
    cw = 512 if n % 512 == 0 else n

    def segment(act):
        xn = xn_ref[...]
        for c0 in range(0, n, cw):
            y = _dot(xn, w_ref[s, :, c0:c0 + cw])
            o_ref[0, :, c0:c0 + cw] = _silu(y) if act else y

    @pl.when(is_silu)
    def _():
        segment(True)

    @pl.when(jnp.logical_not(is_silu))
    def _():
        segment(False)


def _proj(x2d, nw, w_stack, acts, tm):
    T, D = x2d.shape
    S, _, N = w_stack.shape
    return pl.pallas_call(
        functools.partial(_proj_kernel, acts=acts),
        out_shape=jax.ShapeDtypeStruct((S, T, N), F32),
        grid=(T // tm, S),
        in_specs=[pl.BlockSpec((tm, D), lambda m, s: (m, 0)),
                  pl.BlockSpec((1, D), lambda m, s: (0, 0)),
                  pl.BlockSpec((S, D, N), lambda m, s: (0, 0, 0), pipeline_mode=pl.Buffered(1))],
        out_specs=pl.BlockSpec((1, tm, N), lambda m, s: (s, m, 0)),
        scratch_shapes=[pltpu.VMEM((tm, D), BF16)],
        compiler_params=_params(2), name="hgrn_proj",
    )(x2d, nw.reshape(1, D), w_stack)


def _residual_matmul_kernel(a_ref, w_ref, x_ref, o_ref):
    o_ref[...] = x_ref[...] + _dot(a_ref[...], w_ref[...])


def _residual_matmul(a, w, layer, x2d, tm):
    T, K = a.shape
    N = w.shape[2]
    return pl.pallas_call(
        _residual_matmul_kernel,
        out_shape=jax.ShapeDtypeStruct((T, N), F32),
        grid=(T // tm,),
        in_specs=[pl.BlockSpec((tm, K), lambda m: (m, 0)),
                  _resident_layer((K, N), layer),
                  pl.BlockSpec((tm, N), lambda m: (m, 0))],
        out_specs=pl.BlockSpec((tm, N), lambda m: (m, 0)),
        compiler_params=_params(1), name="hgrn_out",
    )(a, w, x2d)


def _forget_lower_bound(raw, layer):
    mx = jnp.max(raw, axis=0, keepdims=True)
    e = jnp.exp(raw - mx)
    sm = e / jnp.sum(e, axis=0, keepdims=True)
    first = sm[0:1]
    cum = first
    for i in range(1, layer + 1):
        cum = cum + sm[i:i + 1]
    return cum - first


def _log_forget_and_key(z, lb):
    t = jnp.exp(-jnp.abs(z))
    r = 1.0 / (1.0 + t)
    sig_neg = jnp.where(z >= 0.0, t * r, r)
    log_sig = jnp.minimum(z, 0.0) - jnp.log(1.0 + t)
    a = jnp.log(lb)
    b = jnp.log1p(-lb) + log_sig
    log_f = jnp.maximum(a, b) + jnp.log(1.0 + jnp.exp(-jnp.abs(a - b)))
    k = (1.0 - lb) * sig_neg
    return log_f, k


def _split3_bf16(x):
    hi = x.astype(BF16)
    r1 = x - hi.astype(F32)
    mid = r1.astype(BF16)
    lo = (r1 - mid.astype(F32)).astype(BF16)
    return hi, mid, lo


def _head_norm_gate(o, ng, gate):
    ms = jnp.mean(o * o, axis=-1, keepdims=True)
    return (o * lax.rsqrt(ms + EPS) * ng * gate).astype(BF16)


def _gla_prompt_kernel(q_ref, z_ref, v_ref, g_ref, lbr_ref, ng_ref, o_ref, s_ref,
                       st_ref, gh_ref, oacc_ref, x_ref, w_ref, ad_ref, *, layer, tl, nt, hg):
    C = min(GLA_CHUNK, tl)
    sub = min(GLA_SUB, C)
    n_sub = C // sub
    W = V7X_LANES
    t_idx = pl.program_id(2)

    @pl.when(t_idx == 0)
    def _():
        st_ref[...] = jnp.zeros(st_ref.shape, F32)
        r = lax.broadcasted_iota(jnp.int32, (sub * W, W), 0) // W
        c = lax.broadcasted_iota(jnp.int32, (sub * W, W), 1)
        for i in range(n_sub):
            w_ref[i] = (c == r + i * sub).astype(BF16)

    lb_all = _forget_lower_bound(lbr_ref[...], layer)
    tri = (lax.broadcasted_iota(jnp.int32, (C, C), 0)
           >= lax.broadcasted_iota(jnp.int32, (C, C), 1)).astype(BF16)
    a_row = lax.broadcasted_iota(jnp.int32, (sub, C), 0)
    a_col = lax.broadcasted_iota(jnp.int32, (sub, C), 1)
    zero_rows = jnp.zeros((V7X_SUBLANES, W), F32)

    n_chunks = tl // C

    def chunk_rows(ci):
        start = ci * C if isinstance(ci, int) else pl.multiple_of(ci * C, C)
        return pl.ds(start, C)

    def decay_terms(ci, slot):
        rows = chunk_rows(ci)
        log_f, k = _log_forget_and_key(z_ref[0, 0, rows, :], lb_all)
        hi, mid, lo = _split3_bf16(log_f)
        G = (_dot(tri, hi) + _dot(tri, mid) + _dot(tri, lo)) * LOG2E
        gh_ref[slot, 0] = G
        gh_ref[slot, 1] = G - jnp.log2(k)
        gh_ref[slot, 2] = k

    def dense(ci, slot):
        rows = chunk_rows(ci)
        for hh in range(hg):
            ls = slice(hh * W, (hh + 1) * W)
            q = q_ref[0, 0, rows, ls]
            v = v_ref[0, 0, rows, ls].astype(BF16)
            G = gh_ref[slot, 0, :, ls]
            k = gh_ref[slot, 2, :, ls]
            g_last = G[C - 1:C, :]
            st = st_ref[hh]
            o = _dot_nt((q * jnp.exp2(G)).astype(BF16), st.astype(BF16))
            off_blocks = [jnp.zeros((sub, C), BF16)]
            for i in range(n_sub):
                base = i * sub
                Gi, qi = G[base:base + sub], q[base:base + sub]
                for s in range(sub):
                    lo_row = (s // V7X_SUBLANES) * V7X_SUBLANES
                    h_s = gh_ref[slot, 1, base + s:base + s + 1, ls]
                    e = jnp.exp2(jnp.minimum(Gi[lo_row:] - h_s, 0.0)) * qi[lo_row:]
                    if lo_row:
                        e = jnp.concatenate([zero_rows] * (lo_row // V7X_SUBLANES) + [e], axis=0)
                    x_ref[i, hh * sub:(hh + 1) * sub, s * W:(s + 1) * W] = e.astype(BF16)
                if i > 0:
                    g_prev = gh_ref[slot, 0, base - 1:base, ls]
                    qt = (qi * jnp.exp2(Gi - g_prev)).astype(BF16)
                    kt = jnp.concatenate(
                        [(k[:base] * jnp.exp2(g_prev - G[:base])).astype(BF16),
                         jnp.zeros((C - base, W), BF16)], axis=0)
                    off_blocks.append(_dot_nt(qt, kt).astype(BF16))
            a_off = jnp.concatenate(off_blocks, axis=0) if n_sub > 1 else off_blocks[0]
            oacc_ref[hh] = o + _dot(a_off, v)
            kd = (k * jnp.exp2(g_last - G)).astype(BF16)
            st_ref[hh] = jnp.exp2(g_last) * st + _dot_tn(v, kd)

    def finish(ci):
        rows = chunk_rows(ci)
        for i in range(n_sub):
            ad_ref[i] = _dot(x_ref[i], w_ref[i])
        for hh in range(hg):
            ls = slice(hh * W, (hh + 1) * W)
            blocks = [jnp.where(a_col <= a_row + i * sub,
                                ad_ref[i, hh * sub:(hh + 1) * sub, 0:C], 0.0).astype(BF16)
                      for i in range(n_sub)]
            a_diag = jnp.concatenate(blocks, axis=0) if n_sub > 1 else blocks[0]
            o = oacc_ref[hh] + _dot(a_diag, v_ref[0, 0, rows, ls].astype(BF16))
            o_ref[0, rows, ls] = _head_norm_gate(o, ng_ref[:, ls], g_ref[0, 0, rows, ls])

    decay_terms(0, 0)

    def chunk(ci, carry):
        slot = ci % 2
        dense(ci, slot)
        finish(ci)
        decay_terms(jnp.minimum(ci + 1, n_chunks - 1), 1 - slot)
        return carry

    lax.fori_loop(0, n_chunks, chunk, 0)

    @pl.when(t_idx == nt - 1)
    def _():
        for hh in range(hg):
            s_ref[0, hh] = st_ref[hh].T


def _gla_prompt(P, lb_raw, layer, ng, n_heads):
    _, B, L, D = P.shape
    W = D // n_heads
    assert W == V7X_LANES
    hg = min(GLA_HEADS_PER_STEP, n_heads)
    tl = min(GLA_TL, L)
    nt = L // tl
    C = min(GLA_CHUNK, tl)
    sub = min(GLA_SUB, C)
    n_sub = C // sub
    NL = lb_raw.shape[0]
    seg = lambda s: pl.BlockSpec((1, 1, tl, hg * W), lambda b, h, t: (s, b, t, h))
    return pl.pallas_call(
        functools.partial(_gla_prompt_kernel, layer=layer, tl=tl, nt=nt, hg=hg),
        out_shape=(jax.ShapeDtypeStruct((B, L, D), BF16),
                   jax.ShapeDtypeStruct((B, n_heads, W, W), F32)),
        grid=(B, n_heads // hg, nt),
        in_specs=[seg(0), seg(1), seg(2), seg(3),
                  pl.BlockSpec((NL, hg * W), lambda b, h, t: (0, h)),
                  pl.BlockSpec((1, hg * W), lambda b, h, t: (0, h))],
        out_specs=(pl.BlockSpec((1, tl, hg * W), lambda b, h, t: (b, t, h)),
                   pl.BlockSpec((1, hg, W, W), lambda b, h, t: (b, h, 0, 0))),
        scratch_shapes=[pltpu.VMEM((hg, W, W), F32),
                        pltpu.VMEM((2, 3, C, hg * W), F32),
                        pltpu.VMEM((hg, C, W), F32),
                        pltpu.VMEM((n_sub, hg * sub, sub * W), BF16),
                        pltpu.VMEM((n_sub, sub * W, W), BF16),
                        pltpu.VMEM((n_sub, hg * sub, W), F32)],
        compiler_params=_params(3), name="gla_prompt",
    )(P, P, P, P, lb_raw, ng.reshape(1, D))


def _gla_sample_kernel(*refs, layer, bb, n_earlier):
    q_ref, z_ref, v_ref, g_ref, lbr_ref, ng_ref, s_ref = refs[:7]
    earlier = refs[7:7 + n_earlier]
    o_ref, ns_all_ref, ft_ref, kt_ref, vb_ref = refs[7 + n_earlier:]
    if n_earlier:
        for l, e_ref in enumerate(earlier):
            ns_all_ref[l] = e_ref[...]
        ns_ref = ns_all_ref.at[n_earlier]
    else:
        ns_ref = ns_all_ref
    j = pl.program_id(1)

    @pl.when(j == 0)
    def _():
        lb = _forget_lower_bound(lbr_ref[...], layer)
        log_f, k = _log_forget_and_key(z_ref[0], lb)
        ft_ref[...] = jnp.exp(log_f).T
        kt_ref[...] = k.T
        vb_ref[...] = v_ref[0].astype(BF16)

    shift = (V7X_LANES - j * bb) % V7X_LANES
    ft = pltpu.roll(ft_ref[...], shift, axis=1)
    kt = kt_ref[...]
    vb = vb_ref[...]
    r0 = pl.multiple_of(j * bb, bb)
    qb = q_ref[0, pl.ds(r0, bb), :].astype(BF16)
    lane_b = lax.broadcasted_iota(jnp.int32, kt.shape, 1)
    row_b = lax.broadcasted_iota(jnp.int32, (bb, vb.shape[1]), 0)
    o_acc = jnp.zeros((bb, vb.shape[1]), F32)
    for i in range(bb):
        k_b = jnp.where(lane_b == r0 + i, kt, 0.0).astype(BF16)
        s_new = ft[:, i:i + 1] * s_ref[i, 0] + _dot(k_b, vb)
        ns_ref[i, 0] = s_new
        o_acc = jnp.where(row_b == i, _dot(qb, s_new.astype(BF16)), o_acc)
    o_ref[...] = _head_norm_gate(o_acc, ng_ref[...], g_ref[0, pl.ds(r0, bb), :])


def _gla_sample(P, state, lb_raw, layer, ng, earlier=()):
    _, NB, D = P.shape
    _, _, H, DK, DV = state.shape
    assert NB == V7X_LANES and DK == V7X_LANES and DV == V7X_LANES
    bb = 32
    NL = lb_raw.shape[0]
    n_earlier = len(earlier)
    seg = lambda s: pl.BlockSpec((1, NB, DK), lambda h, j: (s, 0, h))
    one_state = pl.BlockSpec((bb, 1, DK, DV), lambda h, j: (j, h, 0, 0))
    if n_earlier:
        st_shape = (n_earlier + 1,) + state.shape[1:]
        st_spec = pl.BlockSpec((n_earlier + 1, bb, 1, DK, DV), lambda h, j: (0, j, h, 0, 0))
    else:
        st_shape, st_spec = state.shape[1:], one_state
    return pl.pallas_call(
        functools.partial(_gla_sample_kernel, layer=layer, bb=bb, n_earlier=n_earlier),
        out_shape=(jax.ShapeDtypeStruct((NB, D), BF16), jax.ShapeDtypeStruct(st_shape, F32)),
        grid=(H, NB // bb),
        in_specs=[seg(0), seg(1), seg(2), seg(3),
                  pl.BlockSpec((NL, DK), lambda h, j: (0, h)),
                  pl.BlockSpec((1, DV), lambda h, j: (0, h)),
                  pl.BlockSpec((None, bb, 1, DK, DV), lambda h, j: (layer, j, h, 0, 0))]
                 + [one_state] * n_earlier,
        out_specs=(pl.BlockSpec((bb, DV), lambda h, j: (j, h)), st_spec),
        scratch_shapes=[pltpu.VMEM((DK, NB), F32), pltpu.VMEM((DK, NB), F32),
                        pltpu.VMEM((NB, DV), BF16)],
        compiler_params=_params(2), name="gla_sample",
    )(P, P, P, P, lb_raw, ng.reshape(1, D), state, *earlier)


def kernel(x_prompt, x_sample, state_conv, state_hgrn, state_ffn, norm_mix, norm_ffn, norm_final,
           conv_w_pw1, conv_b_pw1, conv_w_dw, conv_b_dw, conv_ln_g, conv_ln_b, conv_w_pw2,
           conv_b_pw2, hgrn_w_q, hgrn_w_f, hgrn_w_i, hgrn_w_g, hgrn_w_o, hgrn_lb_raw,
           hgrn_norm_g, ffn_w_up, ffn_w_dw, ffn_b_dw, ffn_w_down):
    B, L, D = x_prompt.shape
    NB = x_sample.shape[0]
    depth = ffn_w_up.shape[0]
    n_heads = state_hgrn.shape[2]
    F = ffn_w_down.shape[1]
    lb_raw = hgrn_lb_raw.astype(F32)

    xp = x_prompt.reshape(B * L, D)
    xs = x_sample.reshape(NB, D)
    conv_p, hgrn_p, ffn_p, conv_s, hgrn_s, ffn_s = [], [], [], [], [], []
    hgrn_s_all = conv_s_all = None
    acts = (True, False, False, True)
    w_pw1 = conv_w_pw1.astype(BF16)
    w_pw2 = conv_w_pw2.astype(BF16)
    w_o = hgrn_w_o.astype(BF16)
    w_up = ffn_w_up.astype(BF16)
    w_down = ffn_w_down.astype(BF16)
    state_conv_t = state_conv.transpose(0, 2, 1, 3)
    for i in range(depth):
        j = i // 2
        if i % 2 == 0:
            cargs = (j, norm_mix[i], w_pw1, conv_b_pw1[j], conv_w_dw[j], conv_b_dw[j],
                     conv_ln_g[j], conv_ln_b[j], w_pw2, conv_b_pw2[j])
            xp, st = _conv_prompt(xp, B, L, *cargs)
            conv_p.append(st)
            if j == state_conv.shape[0] - 1 and conv_s:
                xs, conv_s_all = _conv_sample(xs, state_conv_t, *cargs, earlier=tuple(conv_s))
            else:
                xs, st = _conv_sample(xs, state_conv_t, *cargs)
                conv_s.append(st)
        else:
            w_stack = jnp.stack([hgrn_w_q[j], hgrn_w_f[j], hgrn_w_i[j], hgrn_w_g[j]]).astype(BF16)
            P = _proj(xp, norm_mix[i], w_stack, acts, tm=min(PROJ_TM, L))
            og, st = _gla_prompt(P.reshape(4, B, L, D), lb_raw, j, hgrn_norm_g[j], n_heads)
            hgrn_p.append(st)
            xp = _residual_matmul(og.reshape(B * L, D), w_o, j, xp, tm=min(PROJ_TM, L))
            P = _proj(xs, norm_mix[i], w_stack, acts, tm=NB)
            if j == state_hgrn.shape[0] - 1 and hgrn_s:
                og, hgrn_s_all = _gla_sample(P, state_hgrn, lb_raw, j, hgrn_norm_g[j],
                                             earlier=tuple(hgrn_s))
            else:
                og, st = _gla_sample(P, state_hgrn, lb_raw, j, hgrn_norm_g[j])
                hgrn_s.append(st)
            xs = _residual_matmul(og, w_o, j, xs, tm=NB)
        last = i == depth - 1
        fargs = (i, norm_ffn[i], w_up, ffn_w_dw[i], ffn_b_dw[i], w_down, norm_final)
        xp, st = _ffn(xp, L, *fargs, final_norm=last)
        ffn_p.append(st.reshape(B, 2, 2 * F))
        xs, st_gate, st_up = _ffn(xs, 1, *fargs, prev=state_ffn, final_norm=last)
        ffn_s.append((st_gate, st_up))
    return (xp.reshape(B, L, D), xs.reshape(NB, 1, D),
            jnp.stack(conv_p), jnp.stack(hgrn_p), jnp.stack(ffn_p),
            (conv_s_all if conv_s_all is not None else jnp.stack(conv_s)).transpose(0, 2, 1, 3),
            hgrn_s_all if hgrn_s_all is not None else jnp.stack(hgrn_s),
            jnp.concatenate([jnp.stack([g for g, _ in ffn_s]), jnp.stack([u for _, u in ffn_s])],
                            axis=-1))
```
